```python
import math
import jax, jax.numpy as jnp
from jax import lax
import numpy as np

D_MODEL = 2048
BATCH = 16
SEQ = 2048
DEPTH = 1

N_DIFF_HEADS = 8
DIFF_HEAD_DIM = 128
DIFF_QK_HALF = DIFF_HEAD_DIM // 2
N_DSA_HEADS = 8
DSA_HEAD_DIM = 128
N_IDX_HEADS = 16
IDX_HEAD_DIM = 64
DSA_TOPK_MAX = 256
PEER_HEADS = 8
PEER_NKEYS = 128
PEER_N_EXPERTS = PEER_NKEYS * PEER_NKEYS
PEER_QDIM = 256
PEER_HALF = PEER_QDIM // 2
PEER_TOPK = 16
Q_BLOCK = 128
TOKEN_CHUNK = 128
LN_EPS = 1e-5
RMS_EPS = 1e-5
ALPHA = (2 * DEPTH) ** 0.25
BETA = (8 * DEPTH) ** -0.25

DIFF_W = N_DIFF_HEADS * DIFF_HEAD_DIM
DSA_QW = N_DSA_HEADS * DSA_HEAD_DIM
IDX_QW = N_IDX_HEADS * IDX_HEAD_DIM
MIX_W = DIFF_W + DSA_QW
IN_SPLITS = (DIFF_W, DIFF_W, DIFF_W, DSA_QW, DSA_HEAD_DIM, DSA_HEAD_DIM, IDX_QW, IDX_HEAD_DIM, N_IDX_HEADS)
IN_IS_VALUE = (False, False, True, False, False, True, False, False, False)
IN_COLS = sum(IN_SPLITS)

kernel_name = "hymba_diffattn_dsa_peer_deepnorm"


def _alibi_slopes():
    n = N_DIFF_HEADS + N_DSA_HEADS
    s = 2.0 ** (-8.0 * np.arange(1, n + 1) / n)
    return (jnp.asarray(s[0::2], dtype=jnp.float32), jnp.asarray(s[1::2], dtype=jnp.float32))


def _layer_norm(x, g, b):
    xf = x.astype(jnp.float32)
    mu = jnp.mean(xf, axis=-1, keepdims=True)
    var = jnp.mean(jnp.square(xf - mu), axis=-1, keepdims=True)
    return ((xf - mu) * lax.rsqrt(var + LN_EPS) * g.astype(jnp.float32) + b.astype(jnp.float32)).astype(x.dtype)


def _split_columns(proj):
    offsets = [int(o) for o in np.cumsum(IN_SPLITS)[:-1]]
    return jnp.split(proj, offsets, axis=-1)


def _diff_attention(q, k, v, lam, lam_init, subln_g, slopes):
    B, S = q.shape[0], q.shape[1]
    scale = DIFF_QK_HALF ** -0.5
    kpos = jnp.arange(S)

    def block(i):
        qs = lax.dynamic_slice_in_dim(q, i * Q_BLOCK, Q_BLOCK, axis=1)
        qpos = i * Q_BLOCK + jnp.arange(Q_BLOCK)
        logits = jnp.einsum('bqhcd,bshcd->bchqs', qs, k).astype(jnp.float32) * scale
        dist = (qpos[:, None] - kpos[None, :]).astype(jnp.float32)
        logits = jnp.where(dist >= 0, logits - slopes[:, None, None] * dist, -jnp.inf)
        p = jax.nn.softmax(logits, axis=-1)
        w = p[:, 0] - lam * p[:, 1]
        return jnp.einsum('bhqs,bshd->bqhd', w.astype(v.dtype), v)

    out = lax.map(block, jnp.arange(S // Q_BLOCK))
    out = jnp.moveaxis(out, 0, 1).reshape(B, S, N_DIFF_HEADS, DIFF_HEAD_DIM)
    of = out.astype(jnp.float32)
    of = of * lax.rsqrt(jnp.mean(jnp.square(of), axis=-1, keepdims=True) + RMS_EPS)
    of = of * subln_g.astype(jnp.float32) * (1.0 - lam_init)
    return of.astype(v.dtype)


def _dsa_attention(q, k, v, iq, ik, iw, slopes):
    B, S = q.shape[0], q.shape[1]
    topk = min(DSA_TOPK_MAX, S // 4)
    scale = DSA_HEAD_DIM ** -0.5
    idx_scale = (IDX_HEAD_DIM ** -0.5) * (N_IDX_HEADS ** -0.5)
    kpos = jnp.arange(S)
    gather = jax.vmap(lambda t, idx: t[idx])

    def block(i):
        qs = lax.dynamic_slice_in_dim(q, i * Q_BLOCK, Q_BLOCK, axis=1)
        iqs = lax.dynamic_slice_in_dim(iq, i * Q_BLOCK, Q_BLOCK, axis=1)
        iws = lax.dynamic_slice_in_dim(iw, i * Q_BLOCK, Q_BLOCK, axis=1)
        qpos = i * Q_BLOCK + jnp.arange(Q_BLOCK)
        rel = jax.nn.relu(jnp.einsum('bqhd,bsd->bqhs', iqs, ik).astype(jnp.float32))
        iscore = jnp.einsum('bqh,bqhs->bqs', iws.astype(jnp.float32), rel) * idx_scale
        iscore = jnp.where(kpos[None, :] <= qpos[:, None], iscore, -jnp.inf)
        _, sel = lax.top_k(iscore, topk)
        ks = gather(k, sel)
        vs = gather(v, sel)
        logits = jnp.einsum('bqhd,bqkd->bhqk', qs, ks).astype(jnp.float32) * scale
        dist = (qpos[None, :, None] - sel).astype(jnp.float32)
        logits = jnp.where((dist >= 0)[:, None], logits - slopes[None, :, None, None] * dist[:, None], -jnp.inf)
        p = jax.nn.softmax(logits, axis=-1)
        return jnp.einsum('bhqk,bqkd->bqhd', p.astype(vs.dtype), vs)

    out = lax.map(block, jnp.arange(S // Q_BLOCK))
    return jnp.moveaxis(out, 0, 1).reshape(B, S, N_DSA_HEADS, DSA_HEAD_DIM)


def _peer(x, wq, sk1, sk2, u, v):
    B, S, D = x.shape
    xt = x.reshape((B * S) // TOKEN_CHUNK, TOKEN_CHUNK, D)

    def chunk(xc):
        qh = (xc @ wq).reshape(TOKEN_CHUNK, PEER_HEADS, 2, PEER_HALF)
        s1 = jnp.einsum('chd,nd->chn', qh[:, :, 0], sk1).astype(jnp.float32)
        s2 = jnp.einsum('chd,nd->chn', qh[:, :, 1], sk2).astype(jnp.float32)
        v1, i1 = lax.top_k(s1, PEER_TOPK)
        v2, i2 = lax.top_k(s2, PEER_TOPK)
        cand = (v1[..., :, None] + v2[..., None, :]).reshape(TOKEN_CHUNK, PEER_HEADS, PEER_TOPK * PEER_TOPK)
        cidx = (i1[..., :, None] * PEER_NKEYS + i2[..., None, :]).reshape(TOKEN_CHUNK, PEER_HEADS, PEER_TOPK * PEER_TOPK)
        top_s, pos = lax.top_k(cand, PEER_TOPK)
        eidx = jnp.take_along_axis(cidx, pos, axis=-1)
        g = jax.nn.softmax(top_s, axis=-1)
        ue = u[eidx]
        h = jnp.einsum('cd,chkd->chk', xc, ue).astype(jnp.float32)
        a = jax.nn.gelu(h, approximate=False) * g
        ve = v[eidx]
        return jnp.einsum('chk,chkd->cd', a.astype(ve.dtype), ve)

    return lax.map(chunk, xt).reshape(B, S, D)


def setup_inputs(seed: int = 0) -> dict:
    key = jax.random.key(seed)
    ks = jax.random.split(key, 20)
    f32 = jnp.float32
    col_scale = jnp.asarray(np.concatenate([np.full(n, BETA if isv else 1.0) for n, isv in zip(IN_SPLITS, IN_IS_VALUE)]).astype(np.float32))
    x = jax.random.normal(ks[0], (BATCH, SEQ, D_MODEL), f32)
    w_in = jax.random.normal(ks[1], (DEPTH, D_MODEL, IN_COLS), f32) * (D_MODEL ** -0.5) * col_scale
    w_o = jax.random.normal(ks[2], (DEPTH, MIX_W, D_MODEL), f32) * (MIX_W ** -0.5) * BETA
    lambda_q1 = jax.random.normal(ks[3], (DEPTH, DIFF_QK_HALF), f32) * 0.1
    lambda_k1 = jax.random.normal(ks[4], (DEPTH, DIFF_QK_HALF), f32) * 0.1
    lambda_q2 = jax.random.normal(ks[5], (DEPTH, DIFF_QK_HALF), f32) * 0.1
    lambda_k2 = jax.random.normal(ks[6], (DEPTH, DIFF_QK_HALF), f32) * 0.1
    subln_g = 1.0 + 0.02 * jax.random.normal(ks[7], (DEPTH, DIFF_HEAD_DIM), f32)
    ln1_g = 1.0 + 0.02 * jax.random.normal(ks[8], (DEPTH, D_MODEL), f32)
    ln1_b = 0.02 * jax.random.normal(ks[9], (DEPTH, D_MODEL), f32)
    peer_wq = jax.random.normal(ks[10], (DEPTH, D_MODEL, PEER_HEADS * PEER_QDIM), f32) * (D_MODEL ** -0.5)
    peer_k1 = jax.random.normal(ks[11], (DEPTH, PEER_NKEYS, PEER_HALF), f32) * (PEER_HALF ** -0.5)
    peer_k2 = jax.random.normal(ks[12], (DEPTH, PEER_NKEYS, PEER_HALF), f32) * (PEER_HALF ** -0.5)
    peer_u = jax.random.normal(ks[13], (DEPTH, PEER_N_EXPERTS, D_MODEL), f32) * (D_MODEL ** -0.5) * BETA
    peer_v = jax.random.normal(ks[14], (DEPTH, PEER_N_EXPERTS, D_MODEL), f32) * BETA
    ln2_g = 1.0 + 0.02 * jax.random.normal(ks[15], (DEPTH, D_MODEL), f32)
    ln2_b = 0.02 * jax.random.normal(ks[16], (DEPTH, D_MODEL), f32)
    return {"x": x, "w_in": w_in, "w_o": w_o, "lambda_q1": lambda_q1, "lambda_k1": lambda_k1,
            "lambda_q2": lambda_q2, "lambda_k2": lambda_k2, "subln_g": subln_g, "ln1_g": ln1_g,
            "ln1_b": ln1_b, "peer_wq": peer_wq, "peer_k1": peer_k1, "peer_k2": peer_k2,
            "peer_u": peer_u, "peer_v": peer_v, "ln2_g": ln2_g, "ln2_b": ln2_b}


def reference(x, w_in, w_o, lambda_q1, lambda_k1, lambda_q2, lambda_k2, subln_g, ln1_g, ln1_b,
              peer_wq, peer_k1, peer_k2, peer_u, peer_v, ln2_g, ln2_b):
    B, S, _ = x.shape
    slopes_diff, slopes_dsa = _alibi_slopes()
    for l in range(DEPTH):
        proj = x @ w_in[l]
        dq, dk, dv, sq, sk, sv, iq, ik, iw = _split_columns(proj)
        lam_init = 0.8 - 0.6 * math.exp(-0.3 * l)
        lam = (jnp.exp(jnp.sum(lambda_q1[l].astype(jnp.float32) * lambda_k1[l].astype(jnp.float32)))
               - jnp.exp(jnp.sum(lambda_q2[l].astype(jnp.float32) * lambda_k2[l].astype(jnp.float32)))
               + lam_init)
        diff_out = _diff_attention(
            dq.reshape(B, S, N_DIFF_HEADS, 2, DIFF_QK_HALF),
            dk.reshape(B, S, N_DIFF_HEADS, 2, DIFF_QK_HALF),
            dv.reshape(B, S, N_DIFF_HEADS, DIFF_HEAD_DIM),
            lam, lam_init, subln_g[l], slopes_diff)
        dsa_out = _dsa_attention(
            sq.reshape(B, S, N_DSA_HEADS, DSA_HEAD_DIM), sk, sv,
            iq.reshape(B, S, N_IDX_HEADS, IDX_HEAD_DIM), ik, iw, slopes_dsa)
        mixed = jnp.concatenate([diff_out.reshape(B, S, DIFF_W), dsa_out.reshape(B, S, DSA_QW)], axis=-1) @ w_o[l]
        x = _layer_norm(ALPHA * x + mixed, ln1_g[l], ln1_b[l])
        y = _peer(x, peer_wq[l], peer_k1[l], peer_k2[l], peer_u[l], peer_v[l])
        x = _layer_norm(ALPHA * x + y, ln2_g[l], ln2_b[l])
    return x
```

```python
import functools
import math

import jax
import jax.numpy as jnp
import numpy as np
from jax import lax
from jax.experimental import pallas as pl
from jax.experimental.pallas import tpu as pltpu

F32 = jnp.float32
BF16 = jnp.bfloat16

N_DIFF_HEADS = 8
DIFF_HEAD_DIM = 128
DIFF_QK_HALF = DIFF_HEAD_DIM // 2
N_DSA_HEADS = 8
DSA_HEAD_DIM = 128
N_IDX_HEADS = 16
IDX_HEAD_DIM = 64
DSA_TOPK_MAX = 256
PEER_HEADS = 8
PEER_NKEYS = 128
PEER_QDIM = 256
PEER_HALF = PEER_QDIM // 2
PEER_TOPK = 16
LN_EPS = 1e-5
RMS_EPS = 1e-5

DIFF_W = N_DIFF_HEADS * DIFF_HEAD_DIM
DSA_QW = N_DSA_HEADS * DSA_HEAD_DIM
IDX_QW = N_IDX_HEADS * IDX_HEAD_DIM
IN_SPLITS = (DIFF_W, DIFF_W, DIFF_W, DSA_QW, DSA_HEAD_DIM, DSA_HEAD_DIM, IDX_QW, IDX_HEAD_DIM, N_IDX_HEADS)

TB = 256
LANE = 128
FM_ROWS = DIFF_W * 2 + DSA_QW + IDX_QW + DSA_HEAD_DIM
TOK_COLS = DIFF_W + DSA_HEAD_DIM + 2 * IDX_HEAD_DIM
VMEM_LIMIT = 56 * 1024 * 1024

NEG_BIG = -1e30
KEY_NEG_INF = (0xFF800000 ^ 0x7FFFFFFF) - 2 ** 32
INT_MIN = -(2 ** 31)

NT_DIMS = (((1,), (1,)), ((), ()))
TN_DIMS = (((0,), (0,)), ((), ()))


def _alibi_slopes():
    n = N_DIFF_HEADS + N_DSA_HEADS
    s = 2.0 ** (-8.0 * np.arange(1, n + 1) / n)
    return (jnp.asarray(s[0::2], dtype=F32), jnp.asarray(s[1::2], dtype=F32))


def _params(sem):
    return pltpu.CompilerParams(dimension_semantics=sem, vmem_limit_bytes=VMEM_LIMIT)


def _in_proj_kernel(x_ref, wt_ref, wtok_ref, wiw_ref, fm_ref, tok_ref, iw_ref, xb_ref):
    nslab = fm_ref.shape[0]

    @pl.when(pl.program_id(1) == 0)
    def _():
        xb = x_ref[...].astype(BF16)
        xb_ref[...] = xb
        tok_ref[...] = jnp.dot(xb, wtok_ref[...], preferred_element_type=F32).astype(BF16)
        iw = lax.dot_general(wiw_ref[...], xb, NT_DIMS, preferred_element_type=F32)
        for s in range(nslab):
            iw_ref[s] = iw[:, s * TB:(s + 1) * TB]

    r = lax.dot_general(wt_ref[...], xb_ref[...], NT_DIMS, preferred_element_type=F32)
    for s in range(nslab):
        fm_ref[s] = r[:, s * TB:(s + 1) * TB].astype(BF16)


def _in_proj(x2d, wt_fm, w_tok, wt_iw, tm, tn):
    T, D = x2d.shape
    nslab = tm // TB
    return pl.pallas_call(
        _in_proj_kernel,
        grid=(T // tm, FM_ROWS // tn),
        in_specs=[
            pl.BlockSpec((tm, D), lambda i, j: (i, 0)),
            pl.BlockSpec((tn, D), lambda i, j: (j, 0)),
            pl.BlockSpec((D, TOK_COLS), lambda i, j: (0, 0)),
            pl.BlockSpec((N_IDX_HEADS, D), lambda i, j: (0, 0)),
        ],
        out_specs=[
            pl.BlockSpec((nslab, tn, TB), lambda i, j: (i, j, 0)),
            pl.BlockSpec((tm, TOK_COLS), lambda i, j: (i, 0)),
            pl.BlockSpec((nslab, N_IDX_HEADS, TB), lambda i, j: (i, 0, 0)),
        ],
        out_shape=[
            jax.ShapeDtypeStruct((T // TB, FM_ROWS, TB), BF16),
            jax.ShapeDtypeStruct((T, TOK_COLS), BF16),
            jax.ShapeDtypeStruct((T // TB, N_IDX_HEADS, TB), F32),
        ],
        scratch_shapes=[pltpu.VMEM((tm, D), BF16)],
        compiler_params=_params(("parallel", "arbitrary")),
        name="in_proj",
    )(x2d, wt_fm, w_tok, wt_iw)


def _rel_pos():
    kk = lax.broadcasted_iota(jnp.int32, (TB, TB), 0)
    qq = lax.broadcasted_iota(jnp.int32, (TB, TB), 1)
    return (qq - kk).astype(F32)


def _softmax_step(carry, s, vt):
    m, l, acc = carry
    m_new = jnp.maximum(m, jnp.max(s, axis=0, keepdims=True))
    alpha = jnp.exp(m - m_new)
    p = jnp.exp(s - m_new)
    l = alpha * l + jnp.sum(p, axis=0, keepdims=True)
    acc = alpha * acc + jnp.dot(vt, p.astype(BF16), preferred_element_type=F32)
    return m_new, l, acc


def _softmax_init(dim):
    return (jnp.full((1, TB), NEG_BIG, F32), jnp.zeros((1, TB), F32), jnp.zeros((dim, TB), F32))


def _diff_attn_kernel(lam_init, slopes_ref, lamp_ref, qT_ref, k_ref, vT_ref, g_ref, o_ref):
    h = pl.program_id(1)
    qi = pl.program_id(2)
    slope = slopes_ref[h]
    lp = lamp_ref[...]
    lam = (jnp.exp(jnp.sum(lp[0:1] * lp[1:2], axis=1, keepdims=True))
           - jnp.exp(jnp.sum(lp[2:3] * lp[3:4], axis=1, keepdims=True)) + lam_init)

    qT = qT_ref[0]
    row = lax.broadcasted_iota(jnp.int32, qT.shape, 0)
    zero = jnp.zeros_like(qT)
    q_half = (jnp.where(row < DIFF_QK_HALF, qT, zero), jnp.where(row >= DIFF_QK_HALF, qT, zero))
    rel = _rel_pos()

    def step(kj, carry, diag):
        k = k_ref[pl.ds(pl.multiple_of(kj * TB, TB), TB), :]
        vt = vT_ref[kj]
        bias = slope * (rel + ((qi - kj) * TB).astype(F32))
        out = []
        for c in range(2):
            s = jnp.dot(k, q_half[c], preferred_element_type=F32) - bias
            if diag:
                s = jnp.where(rel >= 0, s, NEG_BIG)
            out.append(_softmax_step(carry[c], s, vt))
        return tuple(out)

    init = (_softmax_init(DIFF_HEAD_DIM), _softmax_init(DIFF_HEAD_DIM))
    carry = lax.fori_loop(0, qi, lambda kj, c: step(kj, c, False), init)
    (_, l0, a0), (_, l1, a1) = step(qi, carry, True)
    out = a0 / l0 - lam * (a1 / l1)
    ms = jnp.mean(out * out, axis=0, keepdims=True)
    o_ref[0] = (out * lax.rsqrt(ms + RMS_EPS) * g_ref[...]).astype(BF16)


def _diff_attn(fm, tok, slopes, lam_params, g_col, lam_init, B, S):
    nq = S // TB
    T = B * S
    return pl.pallas_call(
        functools.partial(_diff_attn_kernel, lam_init),
        grid=(B, N_DIFF_HEADS, nq),
        in_specs=[
            pl.BlockSpec(memory_space=pltpu.SMEM),
            pl.BlockSpec((4, DIFF_QK_HALF), lambda b, h, q: (0, 0)),
            pl.BlockSpec((1, DIFF_HEAD_DIM, TB), lambda b, h, q: (b * nq + q, h, 0)),
            pl.BlockSpec((S, DIFF_HEAD_DIM), lambda b, h, q: (b, h)),
            pl.BlockSpec((nq, DIFF_HEAD_DIM, TB), lambda b, h, q: (b, N_DIFF_HEADS + h, 0)),
            pl.BlockSpec((DIFF_HEAD_DIM, 1), lambda b, h, q: (0, 0)),
        ],
        out_specs=pl.BlockSpec((1, DIFF_HEAD_DIM, TB), lambda b, h, q: (b * nq + q, h, 0)),
        out_shape=jax.ShapeDtypeStruct((T // TB, DIFF_W, TB), BF16),
        compiler_params=_params(("parallel", "parallel", "arbitrary")),
        name="diff_attn",
    )(slopes, lam_params, fm, tok, fm, g_col)


def _dsa_attn_kernel(topk, slopes_ref, sqT_ref, iqT_ref, svT_ref, sk_ref, ikk_ref, iwT_ref, o_ref, keys_ref):
    qi = pl.program_id(1)
    rel = _rel_pos()
    iw = iwT_ref[0]
    row = lax.broadcasted_iota(jnp.int32, (2 * IDX_HEAD_DIM, TB), 0)

    def score_block(kj, diag):
        rows = pl.ds(pl.multiple_of(kj * TB, TB), TB)
        ikk = ikk_ref[rows, :]
        acc = jnp.zeros((TB, TB), F32)
        for p in range(N_IDX_HEADS // 2):
            pair = iqT_ref[0, p * 2 * IDX_HEAD_DIM:(p + 1) * 2 * IDX_HEAD_DIM, :]
            zero = jnp.zeros_like(pair)
            for c in range(2):
                qm = jnp.where(row < IDX_HEAD_DIM, pair, zero) if c == 0 else jnp.where(row >= IDX_HEAD_DIM, pair, zero)
                r = jnp.dot(ikk, qm, preferred_element_type=F32)
                hh = 2 * p + c
                acc = acc + iw[hh:hh + 1, :] * jnp.maximum(r, 0.0)
        if diag:
            acc = jnp.where(rel >= 0, acc, -jnp.inf)
        bits = pltpu.bitcast(acc, jnp.int32)
        keys_ref[rows, :] = jnp.where(bits >= 0, bits, bits ^ jnp.int32(0x7FFFFFFF))

    def score_body(kj, c):
        score_block(kj, False)
        return c

    lax.fori_loop(0, qi, score_body, 0)
    score_block(qi, True)

    def count_ge(cand):
        def body(kj, cnt):
            blk = keys_ref[pl.ds(pl.multiple_of(kj * TB, TB), TB), :]
            return cnt + jnp.sum(jnp.where(blk >= cand, 1, 0).astype(jnp.int32), axis=0, keepdims=True)
        return lax.fori_loop(0, qi + 1, body, jnp.zeros((1, TB), jnp.int32))

    lo = jnp.where(count_ge(jnp.zeros((1, TB), jnp.int32)) >= topk, jnp.int32(0), jnp.int32(INT_MIN))

    def bit_body(t, lo):
        cand = lo + lax.shift_left(jnp.int32(1), jnp.int32(30) - t)
        return jnp.where(count_ge(cand) >= topk, cand, lo)

    lo = lax.fori_loop(0, 31, bit_body, lo)
    thr = jnp.maximum(lo, jnp.int32(KEY_NEG_INF + 1))

    for h in range(N_DSA_HEADS):
        qh = sqT_ref[0, h * DSA_HEAD_DIM:(h + 1) * DSA_HEAD_DIM, :]
        slope = slopes_ref[h]

        def step(kj, carry, qh=qh, slope=slope):
            rows = pl.ds(pl.multiple_of(kj * TB, TB), TB)
            bias = slope * (rel + ((qi - kj) * TB).astype(F32))
            s = jnp.dot(sk_ref[rows, :], qh, preferred_element_type=F32) - bias
            s = jnp.where(keys_ref[rows, :] >= thr, s, NEG_BIG)
            return _softmax_step(carry, s, svT_ref[kj])

        _, l, acc = lax.fori_loop(0, qi + 1, step, _softmax_init(DSA_HEAD_DIM))
        o_ref[0, h * DSA_HEAD_DIM:(h + 1) * DSA_HEAD_DIM, :] = (acc / l).astype(BF16)


def _dsa_attn(fm, tok, iwT, slopes, B, S):
    nq = S // TB
    T = B * S
    topk = min(DSA_TOPK_MAX, S // 4)
    sq_blk = (2 * DIFF_W) // DSA_QW
    iq_blk = (2 * DIFF_W + DSA_QW) // IDX_QW
    sv_blk = (2 * DIFF_W + DSA_QW + IDX_QW) // DSA_HEAD_DIM
    sk_blk = DIFF_W // DSA_HEAD_DIM
    return pl.pallas_call(
        functools.partial(_dsa_attn_kernel, topk),
        grid=(B, nq),
        in_specs=[
            pl.BlockSpec(memory_space=pltpu.SMEM),
            pl.BlockSpec((1, DSA_QW, TB), lambda b, q: (b * nq + q, sq_blk, 0)),
            pl.BlockSpec((1, IDX_QW, TB), lambda b, q: (b * nq + q, iq_blk, 0)),
            pl.BlockSpec((nq, DSA_HEAD_DIM, TB), lambda b, q: (b, sv_blk, 0)),
            pl.BlockSpec((S, DSA_HEAD_DIM), lambda b, q: (b, sk_blk)),
            pl.BlockSpec((S, 2 * IDX_HEAD_DIM), lambda b, q: (b, sk_blk + 1)),
            pl.BlockSpec((1, N_IDX_HEADS, TB), lambda b, q: (b * nq + q, 0, 0)),
        ],
        out_specs=pl.BlockSpec((1, DSA_QW, TB), lambda b, q: (b * nq + q, 0, 0)),
        out_shape=jax.ShapeDtypeStruct((T // TB, DSA_QW, TB), BF16),
        scratch_shapes=[pltpu.VMEM((S, TB), jnp.int32)],
        compiler_params=_params(("parallel", "arbitrary")),
        name="dsa_attn",
    )(slopes, fm, fm, fm, tok, tok, iwT)


def _layer_norm_rows(y, g, b):
    mu = jnp.mean(y, axis=-1, keepdims=True)
    d = y - mu
    var = jnp.mean(d * d, axis=-1, keepdims=True)
    return d * lax.rsqrt(var + LN_EPS) * g + b


def _out_proj_kernel(alpha, diffT_ref, dsaT_ref, wo_ref, x_ref, g_ref, b_ref, x1_ref, x1b_ref):
    for s in range(diffT_ref.shape[0]):
        rows = slice(s * TB, (s + 1) * TB)
        mixed = lax.dot_general(diffT_ref[s], wo_ref[0:DIFF_W, :], TN_DIMS, preferred_element_type=F32)
        mixed = mixed + lax.dot_general(dsaT_ref[s], wo_ref[DIFF_W:DIFF_W + DSA_QW, :], TN_DIMS,
                                        preferred_element_type=F32)
        y = _layer_norm_rows(alpha * x_ref[rows, :] + mixed, g_ref[...], b_ref[...])
        x1_ref[rows, :] = y
        x1b_ref[rows, :] = y.astype(BF16)


def _out_proj(diffT, dsaT, wo, x2d, g, b, alpha, tm):
    T, D = x2d.shape
    nslab = tm // TB
    return pl.pallas_call(
        functools.partial(_out_proj_kernel, alpha),
        grid=(T // tm,),
        in_specs=[
            pl.BlockSpec((nslab, DIFF_W, TB), lambda i: (i, 0, 0)),
            pl.BlockSpec((nslab, DSA_QW, TB), lambda i: (i, 0, 0)),
            pl.BlockSpec((DIFF_W + DSA_QW, D), lambda i: (0, 0)),
            pl.BlockSpec((tm, D), lambda i: (i, 0)),
            pl.BlockSpec((1, D), lambda i: (0, 0)),
            pl.BlockSpec((1, D), lambda i: (0, 0)),
        ],
        out_specs=[pl.BlockSpec((tm, D), lambda i: (i, 0)), pl.BlockSpec((tm, D), lambda i: (i, 0))],
        out_shape=[jax.ShapeDtypeStruct((T, D), F32), jax.ShapeDtypeStruct((T, D), BF16)],
        compiler_params=_params(("parallel",)),
        name="out_proj_ln1",
    )(diffT, dsaT, wo, x2d, g, b)


def _top_values(s, k):
    vals = []
    for _ in range(k):
        m = jnp.max(s, axis=0, keepdims=True)
        vals.append(m)
        s = jnp.where(s == m, -jnp.inf, s)
    return vals


def _peer_gate_kernel(x1b_ref, wqT_ref, k1_ref, k2_ref, s1_ref, s2_ref, e1_ref, e2_ref, tau_ref):
    qT = lax.dot_general(wqT_ref[...], x1b_ref[...], NT_DIMS, preferred_element_type=F32).astype(BF16)
    for h in range(PEER_HEADS):
        base = h * PEER_QDIM
        s1 = jnp.dot(k1_ref[...], qT[base:base + PEER_HALF], preferred_element_type=F32)
        s2 = jnp.dot(k2_ref[...], qT[base + PEER_HALF:base + PEER_QDIM], preferred_element_type=F32)
        v1 = _top_values(s1, PEER_TOPK)
        v2 = jnp.concatenate(_top_values(s2, PEER_TOPK), axis=0)
        cand = jnp.concatenate([v + v2 for v in v1], axis=0)
        tau = _top_values(cand, PEER_TOPK)[-1]
        m1, m2 = v1[0], v2[0:1]
        z = jnp.sum(jnp.where(cand >= tau, jnp.exp(cand - (m1 + m2)), 0.0), axis=0, keepdims=True)
        rows = slice(h * PEER_NKEYS, (h + 1) * PEER_NKEYS)
        s1_ref[0, rows, :] = s1
        s2_ref[0, rows, :] = s2
        e1_ref[0, rows, :] = jnp.exp(s1 - m1) / z
        e2_ref[0, rows, :] = jnp.exp(s2 - m2)
        tau_ref[0, h:h + 1, :] = tau


def _peer_gate(x1b, wqT, k1, k2):
    T, D = x1b.shape
    W = PEER_HEADS * PEER_NKEYS
    fm = lambda i: (i, 0, 0)
    return pl.pallas_call(
        _peer_gate_kernel,
        grid=(T // TB,),
        in_specs=[
            pl.BlockSpec((TB, D), lambda i: (i, 0)),
            pl.BlockSpec((PEER_HEADS * PEER_QDIM, D), lambda i: (0, 0)),
            pl.BlockSpec((PEER_NKEYS, PEER_HALF), lambda i: (0, 0)),
            pl.BlockSpec((PEER_NKEYS, PEER_HALF), lambda i: (0, 0)),
        ],
        out_specs=[pl.BlockSpec((1, W, TB), fm)] * 4 + [pl.BlockSpec((1, PEER_HEADS, TB), fm)],
        out_shape=[jax.ShapeDtypeStruct((T // TB, W, TB), F32)] * 4
        + [jax.ShapeDtypeStruct((T // TB, PEER_HEADS, TB), F32)],
        compiler_params=_params(("parallel",)),
        name="peer_gate",
    )(x1b, wqT, k1, k2)


def _peer_dense_kernel(alpha, x1b_ref, u_ref, v_ref, s1_ref, s2_ref, e1_ref, e2_ref, tau_ref, x1_ref,
                       g_ref, b_ref, o_ref, aT_ref):
    ei = pl.program_id(1)
    te = u_ref.shape[0]
    n_i = te // PEER_NKEYS

    @pl.when(ei == 0)
    def _():
        o_ref[...] = jnp.zeros_like(o_ref)

    for s in range(s1_ref.shape[0]):
        rows = slice(s * TB, (s + 1) * TB)
        hT = lax.dot_general(u_ref[...], x1b_ref[rows, :], NT_DIMS, preferred_element_type=F32)
        for ii in range(n_i):
            i = ei * n_i + ii
            gate = jnp.zeros((PEER_NKEYS, TB), F32)
            for hd in range(PEER_HEADS):
                r = pl.ds(hd * PEER_NKEYS + i, 1)
                keys = slice(hd * PEER_NKEYS, (hd + 1) * PEER_NKEYS)
                zsum = s1_ref[s, r, :] + s2_ref[s, keys, :]
                gate = gate + e1_ref[s, r, :] * jnp.where(zsum >= tau_ref[s, hd:hd + 1, :], e2_ref[s, keys, :], 0.0)
            hb = hT[ii * PEER_NKEYS:(ii + 1) * PEER_NKEYS, :]
            act = 0.5 * hb * (1.0 + lax.erf(hb * np.float32(1.0 / math.sqrt(2.0))))
            aT_ref[ii * PEER_NKEYS:(ii + 1) * PEER_NKEYS, :] = (act * gate).astype(BF16)
        o_ref[rows, :] += lax.dot_general(aT_ref[...], v_ref[...], TN_DIMS, preferred_element_type=F32)

    @pl.when(ei == pl.num_programs(1) - 1)
    def _():
        o_ref[...] = _layer_norm_rows(alpha * x1_ref[...] + o_ref[...], g_ref[...], b_ref[...])


def _peer_dense(x1b, x1, u, v, gates, g, b, alpha, tm, te):
    T, D = x1.shape
    E = u.shape[0]
    W = PEER_HEADS * PEER_NKEYS
    nslab = tm // TB
    s1, s2, e1, e2, tau = gates
    tok = lambda i, e: (i, 0)
    fm = lambda i, e: (i, 0, 0)
    return pl.pallas_call(
        functools.partial(_peer_dense_kernel, alpha),
        grid=(T // tm, E // te),
        in_specs=[
            pl.BlockSpec((tm, D), tok),
            pl.BlockSpec((te, D), lambda i, e: (e, 0)),
            pl.BlockSpec((te, D), lambda i, e: (e, 0)),
            pl.BlockSpec((nslab, W, TB), fm),
            pl.BlockSpec((nslab, W, TB), fm),
            pl.BlockSpec((nslab, W, TB), fm),
            pl.BlockSpec((nslab, W, TB), fm),
            pl.BlockSpec((nslab, PEER_HEADS, TB), fm),
            pl.BlockSpec((tm, D), tok),
            pl.BlockSpec((1, D), lambda i, e: (0, 0)),
            pl.BlockSpec((1, D), lambda i, e: (0, 0)),
        ],
        out_specs=pl.BlockSpec((tm, D), tok),
        out_shape=jax.ShapeDtypeStruct((T, D), F32),
        scratch_shapes=[pltpu.VMEM((te, TB), BF16)],
        compiler_params=_params(("parallel", "arbitrary")),
        name="peer_dense_ln2",
    )(x1b, u, v, s1, s2, e1, e2, tau, x1, g, b)


def _tiles(T):
    tm = 2 * TB if T % (2 * TB) == 0 else TB
    return dict(proj_tm=tm, proj_tn=FM_ROWS // 3, outproj_tm=tm, dense_tm=tm, dense_te=4 * PEER_NKEYS)


def kernel(x, w_in, w_o, lambda_q1, lambda_k1, lambda_q2, lambda_k2, subln_g, ln1_g, ln1_b,
           peer_wq, peer_k1, peer_k2, peer_u, peer_v, ln2_g, ln2_b):
    B, S, D = x.shape
    T = B * S
    depth = w_in.shape[0]
    assert S % TB == 0 and w_in.shape[2] == sum(IN_SPLITS)
    alpha = float((2 * depth) ** 0.25)
    slopes_diff, slopes_dsa = _alibi_slopes()
    tiles = _tiles(T)
    offs = np.cumsum((0,) + IN_SPLITS)
    col = lambda w, k: w[:, offs[k]:offs[k + 1]]

    xt = x.reshape(T, D)
    for l in range(depth):
        w = w_in[l]
        dq, dk, dv, sq, sk, sv, iq, ik, iw = (col(w, k) for k in range(9))
        wt_fm = jnp.concatenate(
            [dq * (DIFF_QK_HALF ** -0.5), dv, sq * (DSA_HEAD_DIM ** -0.5), iq, sv], axis=1).T.astype(BF16)
        w_tok = jnp.concatenate([dk, sk, ik, ik], axis=1).astype(BF16)
        wt_iw = (iw * ((IDX_HEAD_DIM ** -0.5) * (N_IDX_HEADS ** -0.5))).T.astype(BF16)
        fm, tok, iwT = _in_proj(xt, wt_fm, w_tok, wt_iw, tiles["proj_tm"], tiles["proj_tn"])

        lam_init = 0.8 - 0.6 * math.exp(-0.3 * l)
        lam_params = jnp.stack([lambda_q1[l], lambda_k1[l], lambda_q2[l], lambda_k2[l]]).astype(F32)
        g_col = (subln_g[l].astype(F32) * (1.0 - lam_init)).reshape(DIFF_HEAD_DIM, 1)
        diffT = _diff_attn(fm, tok, slopes_diff, lam_params, g_col, lam_init, B, S)
        dsaT = _dsa_attn(fm, tok, iwT, slopes_dsa, B, S)

        x1, x1b = _out_proj(diffT, dsaT, w_o[l].astype(BF16), xt, ln1_g[l].reshape(1, D).astype(F32),
                            ln1_b[l].reshape(1, D).astype(F32), alpha, tiles["outproj_tm"])

        gates = _peer_gate(x1b, peer_wq[l].T.astype(BF16), peer_k1[l].astype(BF16), peer_k2[l].astype(BF16))
        xt = _peer_dense(x1b, x1, peer_u[l].astype(BF16), peer_v[l].astype(BF16), gates,
                         ln2_g[l].reshape(1, D).astype(F32), ln2_b[l].reshape(1, D).astype(F32), alpha,
                         tiles["dense_tm"], tiles["dense_te"])
    return xt.reshape(B, S, D)
```

```python
import functools
import math

import jax
import jax.numpy as jnp
import numpy as np
from jax import lax
from jax.experimental import pallas as pl
from jax.experimental.pallas import tpu as pltpu

F32 = jnp.float32
BF16 = jnp.bfloat16

N_DIFF_HEADS = 8
DIFF_HEAD_DIM = 128
DIFF_QK_HALF = DIFF_HEAD_DIM // 2
N_DSA_HEADS = 8
DSA_HEAD_DIM = 128
N_IDX_HEADS = 16
IDX_HEAD_DIM = 64
DSA_TOPK_MAX = 256
PEER_HEADS = 8
PEER_NKEYS = 128
PEER_QDIM = 256
PEER_HALF = PEER_QDIM // 2
PEER_TOPK = 16
LN_EPS = 1e-5
RMS_EPS = 1e-5

DIFF_W = N_DIFF_HEADS * DIFF_HEAD_DIM
DSA_QW = N_DSA_HEADS * DSA_HEAD_DIM
IDX_QW = N_IDX_HEADS * IDX_HEAD_DIM
IN_SPLITS = (DIFF_W, DIFF_W, DIFF_W, DSA_QW, DSA_HEAD_DIM, DSA_HEAD_DIM, IDX_QW, IDX_HEAD_DIM, N_IDX_HEADS)

TB = 256
LANE = 128
FM_ROWS = DIFF_W * 2 + DSA_QW + IDX_QW + DSA_HEAD_DIM
TOK_COLS = DIFF_W + DSA_HEAD_DIM + 2 * IDX_HEAD_DIM
VMEM_LIMIT = 56 * 1024 * 1024

NEG_BIG = -1e30
KEY_NEG_INF = (0xFF800000 ^ 0x7FFFFFFF) - 2 ** 32
INT_MIN = -(2 ** 31)

NT_DIMS = (((1,), (1,)), ((), ()))
TN_DIMS = (((0,), (0,)), ((), ()))


def _alibi_slopes():
    n = N_DIFF_HEADS + N_DSA_HEADS
    s = 2.0 ** (-8.0 * np.arange(1, n + 1) / n)
    return (jnp.asarray(s[0::2], dtype=F32), jnp.asarray(s[1::2], dtype=F32))


def _params(sem):
    return pltpu.CompilerParams(dimension_semantics=sem, vmem_limit_bytes=VMEM_LIMIT)


def _in_proj_kernel(x_ref, wt_ref, wtok_ref, wiw_ref, fm_ref, tok_ref, iw_ref, xb_ref):
    nslab = fm_ref.shape[0]

    @pl.when(pl.program_id(1) == 0)
    def _():
        xb = x_ref[...].astype(BF16)
        xb_ref[...] = xb
        tok_ref[...] = jnp.dot(xb, wtok_ref[...], preferred_element_type=F32).astype(BF16)
        iw = lax.dot_general(wiw_ref[...], xb, NT_DIMS, preferred_element_type=F32)
        for s in range(nslab):
            iw_ref[s] = iw[:, s * TB:(s + 1) * TB]

    r = lax.dot_general(wt_ref[...], xb_ref[...], NT_DIMS, preferred_element_type=F32)
    for s in range(nslab):
        fm_ref[s] = r[:, s * TB:(s + 1) * TB].astype(BF16)


def _in_proj(x2d, wt_fm, w_tok, wt_iw, tm, tn):
    T, D = x2d.shape
    nslab = tm // TB
    return pl.pallas_call(
        _in_proj_kernel,
        grid=(T // tm, FM_ROWS // tn),
        in_specs=[
            pl.BlockSpec((tm, D), lambda i, j: (i, 0)),
            pl.BlockSpec((tn, D), lambda i, j: (j, 0)),
            pl.BlockSpec((D, TOK_COLS), lambda i, j: (0, 0)),
            pl.BlockSpec((N_IDX_HEADS, D), lambda i, j: (0, 0)),
        ],
        out_specs=[
            pl.BlockSpec((nslab, tn, TB), lambda i, j: (i, j, 0)),
            pl.BlockSpec((tm, TOK_COLS), lambda i, j: (i, 0)),
            pl.BlockSpec((nslab, N_IDX_HEADS, TB), lambda i, j: (i, 0, 0)),
        ],
        out_shape=[
            jax.ShapeDtypeStruct((T // TB, FM_ROWS, TB), BF16),
            jax.ShapeDtypeStruct((T, TOK_COLS), BF16),
            jax.ShapeDtypeStruct((T // TB, N_IDX_HEADS, TB), F32),
        ],
        scratch_shapes=[pltpu.VMEM((tm, D), BF16)],
        compiler_params=_params(("parallel", "arbitrary")),
        name="in_proj",
    )(x2d, wt_fm, w_tok, wt_iw)


def _rel_pos():
    kk = lax.broadcasted_iota(jnp.int32, (TB, TB), 0)
    qq = lax.broadcasted_iota(jnp.int32, (TB, TB), 1)
    return (qq - kk).astype(F32)


def _softmax_step(carry, s, vt):
    m, l, acc = carry
    m_new = jnp.maximum(m, jnp.max(s, axis=0, keepdims=True))
    alpha = jnp.exp(m - m_new)
    p = jnp.exp(s - m_new)
    l = alpha * l + jnp.sum(p, axis=0, keepdims=True)
    acc = alpha * acc + jnp.dot(vt, p.astype(BF16), preferred_element_type=F32)
    return m_new, l, acc


def _softmax_init(dim):
    return (jnp.full((1, TB), NEG_BIG, F32), jnp.zeros((1, TB), F32), jnp.zeros((dim, TB), F32))


def _diff_attn_kernel(lam_init, slopes_ref, lamp_ref, qT_ref, k_ref, vT_ref, g_ref, o_ref):
    h = pl.program_id(1)
    qi = pl.program_id(2)
    slope = slopes_ref[h]
    lp = lamp_ref[...]
    lam = (jnp.exp(jnp.sum(lp[0:1] * lp[1:2], axis=1, keepdims=True))
           - jnp.exp(jnp.sum(lp[2:3] * lp[3:4], axis=1, keepdims=True)) + lam_init)

    qT = qT_ref[0]
    row = lax.broadcasted_iota(jnp.int32, qT.shape, 0)
    zero = jnp.zeros_like(qT)
    q_half = (jnp.where(row < DIFF_QK_HALF, qT, zero), jnp.where(row >= DIFF_QK_HALF, qT, zero))
    rel = _rel_pos()

    def step(kj, carry, diag):
        k = k_ref[pl.ds(pl.multiple_of(kj * TB, TB), TB), :]
        vt = vT_ref[kj]
        bias = slope * (rel + ((qi - kj) * TB).astype(F32))
        out = []
        for c in range(2):
            s = jnp.dot(k, q_half[c], preferred_element_type=F32) - bias
            if diag:
                s = jnp.where(rel >= 0, s, NEG_BIG)
            out.append(_softmax_step(carry[c], s, vt))
        return tuple(out)

    init = (_softmax_init(DIFF_HEAD_DIM), _softmax_init(DIFF_HEAD_DIM))
    carry = lax.fori_loop(0, qi, lambda kj, c: step(kj, c, False), init)
    (_, l0, a0), (_, l1, a1) = step(qi, carry, True)
    out = a0 / l0 - lam * (a1 / l1)
    ms = jnp.mean(out * out, axis=0, keepdims=True)
    o_ref[0] = (out * lax.rsqrt(ms + RMS_EPS) * g_ref[...]).astype(BF16)


def _diff_attn(fm, tok, slopes, lam_params, g_col, lam_init, B, S):
    nq = S // TB
    T = B * S
    return pl.pallas_call(
        functools.partial(_diff_attn_kernel, lam_init),
        grid=(B, N_DIFF_HEADS, nq),
        in_specs=[
            pl.BlockSpec(memory_space=pltpu.SMEM),
            pl.BlockSpec((4, DIFF_QK_HALF), lambda b, h, q: (0, 0)),
            pl.BlockSpec((1, DIFF_HEAD_DIM, TB), lambda b, h, q: (b * nq + q, h, 0)),
            pl.BlockSpec((S, DIFF_HEAD_DIM), lambda b, h, q: (b, h)),
            pl.BlockSpec((nq, DIFF_HEAD_DIM, TB), lambda b, h, q: (b, N_DIFF_HEADS + h, 0)),
            pl.BlockSpec((DIFF_HEAD_DIM, 1), lambda b, h, q: (0, 0)),
        ],
        out_specs=pl.BlockSpec((1, DIFF_HEAD_DIM, TB), lambda b, h, q: (b * nq + q, h, 0)),
        out_shape=jax.ShapeDtypeStruct((T // TB, DIFF_W, TB), BF16),
        compiler_params=_params(("parallel", "parallel", "arbitrary")),
        name="diff_attn",
    )(slopes, lam_params, fm, tok, fm, g_col)


def _dsa_attn_kernel(topk, slopes_ref, sqT_ref, iqT_ref, svT_ref, sk_ref, ikk_ref, iwT_ref, o_ref, keys_ref):
    qi = pl.program_id(1)
    rel = _rel_pos()
    iw = iwT_ref[0]
    row = lax.broadcasted_iota(jnp.int32, (2 * IDX_HEAD_DIM, TB), 0)

    def score_block(kj, diag):
        rows = pl.ds(pl.multiple_of(kj * TB, TB), TB)
        ikk = ikk_ref[rows, :]
        acc = jnp.zeros((TB, TB), F32)
        for p in range(N_IDX_HEADS // 2):
            pair = iqT_ref[0, p * 2 * IDX_HEAD_DIM:(p + 1) * 2 * IDX_HEAD_DIM, :]
            zero = jnp.zeros_like(pair)
            for c in range(2):
                qm = jnp.where(row < IDX_HEAD_DIM, pair, zero) if c == 0 else jnp.where(row >= IDX_HEAD_DIM, pair, zero)
                r = jnp.dot(ikk, qm, preferred_element_type=F32)
                hh = 2 * p + c
                acc = acc + iw[hh:hh + 1, :] * jnp.maximum(r, 0.0)
        if diag:
            acc = jnp.where(rel >= 0, acc, -jnp.inf)
        bits = pltpu.bitcast(acc, jnp.int32)
        keys_ref[rows, :] = jnp.where(bits >= 0, bits, bits ^ jnp.int32(0x7FFFFFFF))

    def score_body(kj, c):
        score_block(kj, False)
        return c

    lax.fori_loop(0, qi, score_body, 0)
    score_block(qi, True)

    def count_ge(cand):
        def body(kj, cnt):
            blk = keys_ref[pl.ds(pl.multiple_of(kj * TB, TB), TB), :]
            return cnt + jnp.sum(jnp.where(blk >= cand, 1, 0).astype(jnp.int32), axis=0, keepdims=True)
        return lax.fori_loop(0, qi + 1, body, jnp.zeros((1, TB), jnp.int32))

    lo = jnp.where(count_ge(jnp.zeros((1, TB), jnp.int32)) >= topk, jnp.int32(0), jnp.int32(INT_MIN))

    def bit_body(t, lo):
        cand = lo + lax.shift_left(jnp.int32(1), jnp.int32(30) - t)
        return jnp.where(count_ge(cand) >= topk, cand, lo)

    lo = lax.fori_loop(0, 31, bit_body, lo)
    thr = jnp.maximum(lo, jnp.int32(KEY_NEG_INF + 1))

    for h in range(N_DSA_HEADS):
        qh = sqT_ref[0, h * DSA_HEAD_DIM:(h + 1) * DSA_HEAD_DIM, :]
        slope = slopes_ref[h]

        def step(kj, carry, qh=qh, slope=slope):
            rows = pl.ds(pl.multiple_of(kj * TB, TB), TB)
            bias = slope * (rel + ((qi - kj) * TB).astype(F32))
            s = jnp.dot(sk_ref[rows, :], qh, preferred_element_type=F32) - bias
            s = jnp.where(keys_ref[rows, :] >= thr, s, NEG_BIG)
            return _softmax_step(carry, s, svT_ref[kj])

        _, l, acc = lax.fori_loop(0, qi + 1, step, _softmax_init(DSA_HEAD_DIM))
        o_ref[0, h * DSA_HEAD_DIM:(h + 1) * DSA_HEAD_DIM, :] = (acc / l).astype(BF16)


def _dsa_attn(fm, tok, iwT, slopes, B, S):
    nq = S // TB
    T = B * S
    topk = min(DSA_TOPK_MAX, S // 4)
    sq_blk = (2 * DIFF_W) // DSA_QW
    iq_blk = (2 * DIFF_W + DSA_QW) // IDX_QW
    sv_blk = (2 * DIFF_W + DSA_QW + IDX_QW) // DSA_HEAD_DIM
    sk_blk = DIFF_W // DSA_HEAD_DIM
    return pl.pallas_call(
        functools.partial(_dsa_attn_kernel, topk),
        grid=(B, nq),
        in_specs=[
            pl.BlockSpec(memory_space=pltpu.SMEM),
            pl.BlockSpec((1, DSA_QW, TB), lambda b, q: (b * nq + q, sq_blk, 0)),
            pl.BlockSpec((1, IDX_QW, TB), lambda b, q: (b * nq + q, iq_blk, 0)),
            pl.BlockSpec((nq, DSA_HEAD_DIM, TB), lambda b, q: (b, sv_blk, 0)),
            pl.BlockSpec((S, DSA_HEAD_DIM), lambda b, q: (b, sk_blk)),
            pl.BlockSpec((S, 2 * IDX_HEAD_DIM), lambda b, q: (b, sk_blk + 1)),
            pl.BlockSpec((1, N_IDX_HEADS, TB), lambda b, q: (b * nq + q, 0, 0)),
        ],
        out_specs=pl.BlockSpec((1, DSA_QW, TB), lambda b, q: (b * nq + q, 0, 0)),
        out_shape=jax.ShapeDtypeStruct((T // TB, DSA_QW, TB), BF16),
        scratch_shapes=[pltpu.VMEM((S, TB), jnp.int32)],
        compiler_params=_params(("parallel", "arbitrary")),
        name="dsa_attn",
    )(slopes, fm, fm, fm, tok, tok, iwT)


def _layer_norm_cols(y, g, b):
    mu = jnp.mean(y, axis=0, keepdims=True)
    d = y - mu
    var = jnp.mean(d * d, axis=0, keepdims=True)
    return d * lax.rsqrt(var + LN_EPS) * g + b


def _out_proj_kernel(alpha, diffT_ref, dsaT_ref, woT_ref, x_ref, g_ref, b_ref, x1T_ref, x1Tb_ref):
    nslab = diffT_ref.shape[0]
    attnT = jnp.concatenate(
        [jnp.concatenate([diffT_ref[s], dsaT_ref[s]], axis=0) for s in range(nslab)], axis=1)
    mixedT = jnp.dot(woT_ref[...], attnT, preferred_element_type=F32)
    for s in range(nslab):
        xT = x_ref[s * TB:(s + 1) * TB, :].T
        y = _layer_norm_cols(alpha * xT + mixedT[:, s * TB:(s + 1) * TB], g_ref[...], b_ref[...])
        x1T_ref[s] = y
        x1Tb_ref[s] = y.astype(BF16)


def _out_proj(diffT, dsaT, woT, x2d, g, b, alpha, tm):
    T, D = x2d.shape
    nslab = tm // TB
    fm = lambda i: (i, 0, 0)
    return pl.pallas_call(
        functools.partial(_out_proj_kernel, alpha),
        grid=(T // tm,),
        in_specs=[
            pl.BlockSpec((nslab, DIFF_W, TB), fm),
            pl.BlockSpec((nslab, DSA_QW, TB), fm),
            pl.BlockSpec((D, DIFF_W + DSA_QW), lambda i: (0, 0)),
            pl.BlockSpec((tm, D), lambda i: (i, 0)),
            pl.BlockSpec((D, 1), lambda i: (0, 0)),
            pl.BlockSpec((D, 1), lambda i: (0, 0)),
        ],
        out_specs=[pl.BlockSpec((nslab, D, TB), fm), pl.BlockSpec((nslab, D, TB), fm)],
        out_shape=[jax.ShapeDtypeStruct((T // TB, D, TB), F32), jax.ShapeDtypeStruct((T // TB, D, TB), BF16)],
        compiler_params=_params(("parallel",)),
        name="out_proj_ln1",
    )(diffT, dsaT, woT, x2d, g, b)


def _top_values(s, k):
    vals = []
    for _ in range(k):
        m = jnp.max(s, axis=0, keepdims=True)
        vals.append(m)
        s = jnp.where(s == m, -jnp.inf, s)
    return vals


def _peer_gate_kernel(x1Tb_ref, wqT_ref, k1_ref, k2_ref, c_ref, s2_ref, e1_ref, e2_ref):
    nslab = x1Tb_ref.shape[0]
    x1cat = jnp.concatenate([x1Tb_ref[s] for s in range(nslab)], axis=1)
    qT = jnp.dot(wqT_ref[...], x1cat, preferred_element_type=F32).astype(BF16)
    for h in range(PEER_HEADS):
        base = h * PEER_QDIM
        s1w = jnp.dot(k1_ref[...], qT[base:base + PEER_HALF], preferred_element_type=F32)
        s2w = jnp.dot(k2_ref[...], qT[base + PEER_HALF:base + PEER_QDIM], preferred_element_type=F32)
        for s in range(nslab):
            s1 = s1w[:, s * TB:(s + 1) * TB]
            s2 = s2w[:, s * TB:(s + 1) * TB]
            v1 = _top_values(s1, PEER_TOPK)
            v2 = _top_values(s2, PEER_TOPK)
            v2c = jnp.concatenate(v2, axis=0)
            cand = jnp.concatenate([v + v2c for v in v1], axis=0)
            tau = _top_values(cand, PEER_TOPK)[-1]
            m1, m2 = v1[0], v2[0]
            z = jnp.sum(jnp.where(cand >= tau, jnp.exp(cand - (m1 + m2)), 0.0), axis=0, keepdims=True)
            c = jnp.full(s1.shape, jnp.inf, F32)
            for vb in v2:
                c = jnp.where(s1 + vb >= tau, vb, c)
            rows = slice(h * PEER_NKEYS, (h + 1) * PEER_NKEYS)
            c_ref[s, rows, :] = c
            s2_ref[s, rows, :] = s2
            e1_ref[s, rows, :] = jnp.exp(s1 - m1) / z
            e2_ref[s, rows, :] = jnp.exp(s2 - m2)


def _peer_gate(x1Tb, wqT, k1, k2, tm):
    nblk, D, _ = x1Tb.shape
    W = PEER_HEADS * PEER_NKEYS
    nslab = tm // TB
    fm = lambda i: (i, 0, 0)
    return pl.pallas_call(
        _peer_gate_kernel,
        grid=(nblk // nslab,),
        in_specs=[
            pl.BlockSpec((nslab, D, TB), fm),
            pl.BlockSpec((PEER_HEADS * PEER_QDIM, D), lambda i: (0, 0)),
            pl.BlockSpec((PEER_NKEYS, PEER_HALF), lambda i: (0, 0)),
            pl.BlockSpec((PEER_NKEYS, PEER_HALF), lambda i: (0, 0)),
        ],
        out_specs=[pl.BlockSpec((nslab, W, TB), fm)] * 4,
        out_shape=[jax.ShapeDtypeStruct((nblk, W, TB), F32)] * 4,
        compiler_params=_params(("parallel",)),
        name="peer_gate",
    )(x1Tb, wqT, k1, k2)


def _peer_gate_units(i0, c_ref, s2_ref, e1_ref, e2_ref, g_ref):
    def unit(ii, s):
        def run():
            i = jnp.minimum(i0 + ii, PEER_NKEYS - 1)
            gate = jnp.zeros((PEER_NKEYS, TB), F32)
            for hd in range(PEER_HEADS):
                r = pl.ds(hd * PEER_NKEYS + i, 1)
                keys = slice(hd * PEER_NKEYS, (hd + 1) * PEER_NKEYS)
                sel = jnp.where(s2_ref[s, keys, :] >= c_ref[s, r, :], e2_ref[s, keys, :], 0.0)
                gate = gate + e1_ref[s, r, :] * sel
            g_ref[ii * PEER_NKEYS:(ii + 1) * PEER_NKEYS, s * TB:(s + 1) * TB] = gate
        return run
    return [unit(ii, s) for ii in range(g_ref.shape[0] // PEER_NKEYS) for s in range(c_ref.shape[0])]


def _peer_dense_kernel(x1Tb_ref, u_ref, vT_ref, c_ref, s2_ref, e1_ref, e2_ref, yT_ref,
                       ga_ref, gb_ref, h_ref, a_ref):
    ei = pl.program_id(1)
    te, tm = ga_ref.shape
    n_i = te // PEER_NKEYS
    nslab = c_ref.shape[0]
    D = u_ref.shape[1]
    kc = D // 4
    rc = D // 4
    gate_args = (c_ref, s2_ref, e1_ref, e2_ref)

    @pl.when(ei == 0)
    def _():
        yT_ref[...] = jnp.zeros_like(yT_ref)
        for unit in _peer_gate_units(0, *gate_args, ga_ref):
            unit()

    def h_unit(t, k):
        def run():
            xk = jnp.concatenate([x1Tb_ref[s, k * kc:(k + 1) * kc, :] for s in range(nslab)], axis=1)
            part = jnp.dot(u_ref[t * te:(t + 1) * te, k * kc:(k + 1) * kc], xk, preferred_element_type=F32)
            if k == 0:
                h_ref[t] = part
            else:
                h_ref[t] += part
        return run

    def act_unit(t, g_ref):
        def run():
            hT = h_ref[t]
            act = 0.5 * hT * (1.0 + lax.erf(hT * np.float32(1.0 / math.sqrt(2.0))))
            a_ref[t] = (act * g_ref[...]).astype(BF16)
        return run

    def y_unit(t, r):
        def run():
            y = jnp.dot(vT_ref[r * rc:(r + 1) * rc, t * te:(t + 1) * te], a_ref[t], preferred_element_type=F32)
            for s in range(nslab):
                yT_ref[s, r * rc:(r + 1) * rc, :] += y[:, s * TB:(s + 1) * TB]
        return run

    gb = _peer_gate_units((2 * ei + 1) * n_i, *gate_args, gb_ref)
    ga = _peer_gate_units((2 * ei + 2) * n_i, *gate_args, ga_ref)
    hA, hB = [h_unit(0, k) for k in range(4)], [h_unit(1, k) for k in range(4)]
    yA, yB = [y_unit(0, r) for r in range(4)], [y_unit(1, r) for r in range(4)]
    actA, actB = act_unit(0, ga_ref), act_unit(1, gb_ref)
    assert len(gb) == 8
    order = [hA[0], gb[0], hA[1], gb[1], hA[2], gb[2], hA[3], gb[3],
             hB[0], actA, hB[1], gb[4], hB[2], gb[5], hB[3], gb[6],
             yA[0], gb[7], yA[1], actB, yA[2], ga[0], yA[3], ga[1],
             yB[0], ga[2], ga[3], yB[1], ga[4], ga[5], yB[2], ga[6], yB[3], ga[7]]
    for unit in order:
        unit()


def _peer_dense(x1Tb, u, vT, gates, tm, te):
    nblk, D, _ = x1Tb.shape
    E = u.shape[0]
    W = PEER_HEADS * PEER_NKEYS
    nslab = tm // TB
    c, s2, e1, e2 = gates
    fm = lambda i, e: (i, 0, 0)
    return pl.pallas_call(
        _peer_dense_kernel,
        grid=(nblk // nslab, E // (2 * te)),
        in_specs=[
            pl.BlockSpec((nslab, D, TB), fm),
            pl.BlockSpec((2 * te, D), lambda i, e: (e, 0)),
            pl.BlockSpec((D, 2 * te), lambda i, e: (0, e)),
            pl.BlockSpec((nslab, W, TB), fm),
            pl.BlockSpec((nslab, W, TB), fm),
            pl.BlockSpec((nslab, W, TB), fm),
            pl.BlockSpec((nslab, W, TB), fm),
        ],
        out_specs=pl.BlockSpec((nslab, D, TB), fm),
        out_shape=jax.ShapeDtypeStruct((nblk, D, TB), F32),
        scratch_shapes=[pltpu.VMEM((te, tm), F32), pltpu.VMEM((te, tm), F32),
                        pltpu.VMEM((2, te, tm), F32), pltpu.VMEM((2, te, tm), BF16)],
        compiler_params=_params(("parallel", "arbitrary")),
        name="peer_dense",
    )(x1Tb, u, vT, c, s2, e1, e2)


def _ln2_kernel(alpha, x1T_ref, yT_ref, g_ref, b_ref, o_ref):
    for s in range(x1T_ref.shape[0]):
        z = _layer_norm_cols(alpha * x1T_ref[s] + yT_ref[s], g_ref[...], b_ref[...])
        o_ref[s * TB:(s + 1) * TB, :] = z.T


def _ln2(x1T, yT, g, b, alpha, tm):
    nblk, D, _ = x1T.shape
    nslab = tm // TB
    fm = lambda i: (i, 0, 0)
    return pl.pallas_call(
        functools.partial(_ln2_kernel, alpha),
        grid=(nblk // nslab,),
        in_specs=[
            pl.BlockSpec((nslab, D, TB), fm),
            pl.BlockSpec((nslab, D, TB), fm),
            pl.BlockSpec((D, 1), lambda i: (0, 0)),
            pl.BlockSpec((D, 1), lambda i: (0, 0)),
        ],
        out_specs=pl.BlockSpec((tm, D), lambda i: (i, 0)),
        out_shape=jax.ShapeDtypeStruct((nblk * TB, D), F32),
        compiler_params=_params(("parallel",)),
        name="ln2_out",
    )(x1T, yT, g, b)


def _tiles(T):
    tm = 2 * TB if T % (2 * TB) == 0 else TB
    return dict(proj_tm=tm, proj_tn=FM_ROWS // 3, outproj_tm=tm, dense_tm=tm, dense_te=4 * PEER_NKEYS)


def kernel(x, w_in, w_o, lambda_q1, lambda_k1, lambda_q2, lambda_k2, subln_g, ln1_g, ln1_b,
           peer_wq, peer_k1, peer_k2, peer_u, peer_v, ln2_g, ln2_b):
    B, S, D = x.shape
    T = B * S
    depth = w_in.shape[0]
    assert S % TB == 0 and w_in.shape[2] == sum(IN_SPLITS)
    alpha = float((2 * depth) ** 0.25)
    slopes_diff, slopes_dsa = _alibi_slopes()
    tiles = _tiles(T)
    offs = np.cumsum((0,) + IN_SPLITS)
    col = lambda w, k: w[:, offs[k]:offs[k + 1]]

    xt = x.reshape(T, D)
    for l in range(depth):
        w = w_in[l]
        dq, dk, dv, sq, sk, sv, iq, ik, iw = (col(w, k) for k in range(9))
        wt_fm = jnp.concatenate(
            [dq * (DIFF_QK_HALF ** -0.5), dv, sq * (DSA_HEAD_DIM ** -0.5), iq, sv], axis=1).T.astype(BF16)
        w_tok = jnp.concatenate([dk, sk, ik, ik], axis=1).astype(BF16)
        wt_iw = (iw * ((IDX_HEAD_DIM ** -0.5) * (N_IDX_HEADS ** -0.5))).T.astype(BF16)
        fm, tok, iwT = _in_proj(xt, wt_fm, w_tok, wt_iw, tiles["proj_tm"], tiles["proj_tn"])

        lam_init = 0.8 - 0.6 * math.exp(-0.3 * l)
        lam_params = jnp.stack([lambda_q1[l], lambda_k1[l], lambda_q2[l], lambda_k2[l]]).astype(F32)
        g_col = (subln_g[l].astype(F32) * (1.0 - lam_init)).reshape(DIFF_HEAD_DIM, 1)
        diffT = _diff_attn(fm, tok, slopes_diff, lam_params, g_col, lam_init, B, S)
        dsaT = _dsa_attn(fm, tok, iwT, slopes_dsa, B, S)

        colv = lambda p: p.reshape(D, 1).astype(F32)
        x1T, x1Tb = _out_proj(diffT, dsaT, w_o[l].T.astype(BF16), xt, colv(ln1_g[l]), colv(ln1_b[l]), alpha,
                              tiles["outproj_tm"])

        gates = _peer_gate(x1Tb, peer_wq[l].T.astype(BF16), peer_k1[l].astype(BF16), peer_k2[l].astype(BF16),
                           tiles["outproj_tm"])
        yT = _peer_dense(x1Tb, peer_u[l].astype(BF16), peer_v[l].T.astype(BF16), gates,
                         tiles["dense_tm"], tiles["dense_te"])
        xt = _ln2(x1T, yT, colv(ln2_g[l]), colv(ln2_b[l]), alpha, tiles["outproj_tm"])
    return xt.reshape(B, S, D)
```

```python
import functools
import math

import jax
import jax.numpy as jnp
import numpy as np
from jax import lax
from jax.experimental import pallas as pl
from jax.experimental.pallas import tpu as pltpu

F32 = jnp.float32
BF16 = jnp.bfloat16

N_DIFF_HEADS = 8
DIFF_HEAD_DIM = 128
DIFF_QK_HALF = DIFF_HEAD_DIM // 2
N_DSA_HEADS = 8
DSA_HEAD_DIM = 128
N_IDX_HEADS = 16
IDX_HEAD_DIM = 64
DSA_TOPK_MAX = 256
PEER_HEADS = 8
PEER_NKEYS = 128
PEER_QDIM = 256
PEER_HALF = PEER_QDIM // 2
PEER_TOPK = 16
LN_EPS = 1e-5
RMS_EPS = 1e-5

DIFF_W = N_DIFF_HEADS * DIFF_HEAD_DIM
DSA_QW = N_DSA_HEADS * DSA_HEAD_DIM
IDX_QW = N_IDX_HEADS * IDX_HEAD_DIM
IN_SPLITS = (DIFF_W, DIFF_W, DIFF_W, DSA_QW, DSA_HEAD_DIM, DSA_HEAD_DIM, IDX_QW, IDX_HEAD_DIM, N_IDX_HEADS)

TB = 256
LANE = 128
FM_ROWS = DIFF_W * 2 + DSA_QW + IDX_QW + DSA_HEAD_DIM
TOK_COLS = DIFF_W + DSA_HEAD_DIM + 2 * IDX_HEAD_DIM
VMEM_LIMIT = 56 * 1024 * 1024

NEG_BIG = -1e30
KEY_NEG_INF = (0xFF800000 ^ 0x7FFFFFFF) - 2 ** 32
INT_MIN = -(2 ** 31)

NT_DIMS = (((1,), (1,)), ((), ()))
TN_DIMS = (((0,), (0,)), ((), ()))


def _alibi_slopes():
    n = N_DIFF_HEADS + N_DSA_HEADS
    s = 2.0 ** (-8.0 * np.arange(1, n + 1) / n)
    return (jnp.asarray(s[0::2], dtype=F32), jnp.asarray(s[1::2], dtype=F32))


def _params(sem):
    return pltpu.CompilerParams(dimension_semantics=sem, vmem_limit_bytes=VMEM_LIMIT)


def _in_proj_kernel(x_ref, wt_ref, wtok_ref, wiw_ref, fm_ref, tok_ref, iw_ref, xb_ref):
    nslab = fm_ref.shape[0]

    @pl.when(pl.program_id(1) == 0)
    def _():
        xb = x_ref[...].astype(BF16)
        xb_ref[...] = xb
        tok_ref[...] = jnp.dot(xb, wtok_ref[...], preferred_element_type=F32).astype(BF16)
        iw = lax.dot_general(wiw_ref[...], xb, NT_DIMS, preferred_element_type=F32)
        for s in range(nslab):
            iw_ref[s] = iw[:, s * TB:(s + 1) * TB]

    r = lax.dot_general(wt_ref[...], xb_ref[...], NT_DIMS, preferred_element_type=F32)
    for s in range(nslab):
        fm_ref[s] = r[:, s * TB:(s + 1) * TB].astype(BF16)


def _in_proj(x2d, wt_fm, w_tok, wt_iw, tm, tn):
    T, D = x2d.shape
    nslab = tm // TB
    return pl.pallas_call(
        _in_proj_kernel,
        grid=(T // tm, FM_ROWS // tn),
        in_specs=[
            pl.BlockSpec((tm, D), lambda i, j: (i, 0)),
            pl.BlockSpec((tn, D), lambda i, j: (j, 0)),
            pl.BlockSpec((D, TOK_COLS), lambda i, j: (0, 0)),
            pl.BlockSpec((N_IDX_HEADS, D), lambda i, j: (0, 0)),
        ],
        out_specs=[
            pl.BlockSpec((nslab, tn, TB), lambda i, j: (i, j, 0)),
            pl.BlockSpec((tm, TOK_COLS), lambda i, j: (i, 0)),
            pl.BlockSpec((nslab, N_IDX_HEADS, TB), lambda i, j: (i, 0, 0)),
        ],
        out_shape=[
            jax.ShapeDtypeStruct((T // TB, FM_ROWS, TB), BF16),
            jax.ShapeDtypeStruct((T, TOK_COLS), BF16),
            jax.ShapeDtypeStruct((T // TB, N_IDX_HEADS, TB), F32),
        ],
        scratch_shapes=[pltpu.VMEM((tm, D), BF16)],
        compiler_params=_params(("parallel", "arbitrary")),
        name="in_proj",
    )(x2d, wt_fm, w_tok, wt_iw)


def _rel_pos():
    kk = lax.broadcasted_iota(jnp.int32, (TB, TB), 0)
    qq = lax.broadcasted_iota(jnp.int32, (TB, TB), 1)
    return (qq - kk).astype(F32)


def _softmax_reset(m_ref, l_ref, acc_ref):
    m_ref[...] = jnp.full(m_ref.shape, NEG_BIG, F32)
    l_ref[...] = jnp.zeros(l_ref.shape, F32)
    acc_ref[...] = jnp.zeros(acc_ref.shape, F32)


def _softmax_update(idx, s, shift, vt, m_ref, l_ref, acc_ref):
    m_old = m_ref[idx]
    m_new = jnp.maximum(m_old, jnp.max(s, axis=0, keepdims=True) - shift)
    alpha = jnp.exp(m_old - m_new)
    p = jnp.exp(s - (m_new + shift))
    l_ref[idx] = alpha * l_ref[idx] + jnp.sum(p, axis=0, keepdims=True)
    acc_ref[idx] = alpha * acc_ref[idx] + jnp.dot(vt, p.astype(BF16), preferred_element_type=F32)
    m_ref[idx] = m_new


def _diff_attn_kernel(lam_init, slopes_ref, lamp_ref, qT_ref, k_ref, vT_ref, g_ref, o_ref,
                      qm_ref, bias_ref, m_ref, l_ref, acc_ref):
    qi = pl.program_id(1)
    lp = lamp_ref[...]
    lam = (jnp.exp(jnp.sum(lp[0:1] * lp[1:2], axis=1, keepdims=True))
           - jnp.exp(jnp.sum(lp[2:3] * lp[3:4], axis=1, keepdims=True)) + lam_init)
    rel = _rel_pos()
    row = lax.broadcasted_iota(jnp.int32, (DIFF_HEAD_DIM, TB), 0)
    hd = lambda h: slice(h * DIFF_HEAD_DIM, (h + 1) * DIFF_HEAD_DIM)

    _softmax_reset(m_ref, l_ref, acc_ref)
    rel2 = jnp.concatenate([rel, rel], axis=1)
    for h in range(N_DIFF_HEADS):
        qT = qT_ref[0, hd(h), :]
        zero = jnp.zeros_like(qT)
        qm_ref[h] = jnp.concatenate(
            [jnp.where(row < DIFF_QK_HALF, qT, zero), jnp.where(row >= DIFF_QK_HALF, qT, zero)], axis=1)
        bias_ref[h] = slopes_ref[h] * rel2

    def step(kj, diag):
        rows = pl.ds(pl.multiple_of(kj * TB, TB), TB)
        dist0 = ((qi - kj) * TB).astype(F32)
        for h in range(N_DIFF_HEADS):
            s = jnp.dot(k_ref[rows, hd(h)], qm_ref[h], preferred_element_type=F32) - bias_ref[h]
            if diag:
                s = jnp.where(rel2 >= 0, s, NEG_BIG)
            _softmax_update(h, s, slopes_ref[h] * dist0, vT_ref[kj, hd(h), :], m_ref, l_ref, acc_ref)

    def body(kj, c):
        step(kj, False)
        return c

    lax.fori_loop(0, qi, body, 0)
    step(qi, True)
    for h in range(N_DIFF_HEADS):
        w = acc_ref[h] / l_ref[h]
        out = w[:, :TB] - lam * w[:, TB:]
        ms = jnp.mean(out * out, axis=0, keepdims=True)
        o_ref[0, hd(h), :] = (out * lax.rsqrt(ms + RMS_EPS) * g_ref[...]).astype(BF16)


def _diff_attn(fm, tok, slopes, lam_params, g_col, lam_init, B, S):
    nq = S // TB
    T = B * S
    nchain = 2 * N_DIFF_HEADS
    return pl.pallas_call(
        functools.partial(_diff_attn_kernel, lam_init),
        grid=(B, nq),
        in_specs=[
            pl.BlockSpec(memory_space=pltpu.SMEM),
            pl.BlockSpec((4, DIFF_QK_HALF), lambda b, q: (0, 0)),
            pl.BlockSpec((1, DIFF_W, TB), lambda b, q: (b * nq + q, 0, 0)),
            pl.BlockSpec((S, DIFF_W), lambda b, q: (b, 0)),
            pl.BlockSpec((nq, DIFF_W, TB), lambda b, q: (b, 1, 0)),
            pl.BlockSpec((DIFF_HEAD_DIM, 1), lambda b, q: (0, 0)),
        ],
        out_specs=pl.BlockSpec((1, DIFF_W, TB), lambda b, q: (b * nq + q, 0, 0)),
        out_shape=jax.ShapeDtypeStruct((T // TB, DIFF_W, TB), BF16),
        scratch_shapes=[
            pltpu.VMEM((N_DIFF_HEADS, DIFF_HEAD_DIM, 2 * TB), BF16),
            pltpu.VMEM((N_DIFF_HEADS, TB, 2 * TB), F32),
            pltpu.VMEM((N_DIFF_HEADS, 1, 2 * TB), F32),
            pltpu.VMEM((N_DIFF_HEADS, 1, 2 * TB), F32),
            pltpu.VMEM((N_DIFF_HEADS, DIFF_HEAD_DIM, 2 * TB), F32),
        ],
        compiler_params=_params(("parallel", "arbitrary")),
        name="diff_attn",
    )(slopes, lam_params, fm, tok, fm, g_col)


def _dsa_attn_kernel(topk, slopes_ref, sqT_ref, iqT_ref, svT_ref, sk_ref, ikk_ref, iwT_ref, o_ref, keys_ref,
                     iqall_ref, bias_ref, qall_ref, m_ref, l_ref, acc_ref):
    qi = pl.program_id(1)
    rel = _rel_pos()
    iw = iwT_ref[0]
    row = lax.broadcasted_iota(jnp.int32, (2 * IDX_HEAD_DIM, TB), 0)

    for p in range(N_IDX_HEADS // 2):
        pair = iqT_ref[0, p * 2 * IDX_HEAD_DIM:(p + 1) * 2 * IDX_HEAD_DIM, :]
        zero = jnp.zeros_like(pair)
        iqall_ref[:, (2 * p) * TB:(2 * p + 1) * TB] = jnp.where(row < IDX_HEAD_DIM, pair, zero)
        iqall_ref[:, (2 * p + 1) * TB:(2 * p + 2) * TB] = jnp.where(row >= IDX_HEAD_DIM, pair, zero)

    def score_block(kj, diag):
        rows = pl.ds(pl.multiple_of(kj * TB, TB), TB)
        ikk = ikk_ref[rows, :]
        acc = jnp.zeros((TB, TB), F32)
        half = N_IDX_HEADS // 2
        for g in range(2):
            r = jnp.dot(ikk, iqall_ref[:, g * half * TB:(g + 1) * half * TB], preferred_element_type=F32)
            for j in range(half):
                hh = g * half + j
                acc = acc + iw[hh:hh + 1, :] * jnp.maximum(r[:, j * TB:(j + 1) * TB], 0.0)
        if diag:
            acc = jnp.where(rel >= 0, acc, -jnp.inf)
        bits = pltpu.bitcast(acc, jnp.int32)
        keys_ref[rows, :] = jnp.where(bits >= 0, bits, bits ^ jnp.int32(0x7FFFFFFF))

    def score_body(kj, c):
        score_block(kj, False)
        return c

    lax.fori_loop(0, qi, score_body, 0)
    score_block(qi, True)

    def count_ge(cand):
        def body(kj, cnt):
            blk = keys_ref[pl.ds(pl.multiple_of(kj * TB, TB), TB), :]
            return cnt + jnp.sum(jnp.where(blk >= cand, 1, 0).astype(jnp.int32), axis=0, keepdims=True)
        return lax.fori_loop(0, qi + 1, body, jnp.zeros((1, TB), jnp.int32))

    lo = jnp.where(count_ge(jnp.zeros((1, TB), jnp.int32)) >= topk, jnp.int32(0), jnp.int32(INT_MIN))

    def bit_body(t, lo):
        cand = lo + lax.shift_left(jnp.int32(1), jnp.int32(30) - t)
        return jnp.where(count_ge(cand) >= topk, cand, lo)

    lo = lax.fori_loop(0, 31, bit_body, lo)
    thr = jnp.maximum(lo, jnp.int32(KEY_NEG_INF + 1))

    hd = lambda h: slice(h * DSA_HEAD_DIM, (h + 1) * DSA_HEAD_DIM)
    lanes = lambda h: slice(h * TB, (h + 1) * TB)
    _softmax_reset(m_ref, l_ref, acc_ref)
    for h in range(N_DSA_HEADS):
        bias_ref[:, lanes(h)] = slopes_ref[h] * rel
        qall_ref[:, lanes(h)] = sqT_ref[0, hd(h), :]
    slope_row = jnp.concatenate([jnp.full((1, TB), slopes_ref[h], F32) for h in range(N_DSA_HEADS)], axis=1)

    def attend(kj, c):
        rows = pl.ds(pl.multiple_of(kj * TB, TB), TB)
        dist0 = ((qi - kj) * TB).astype(F32)
        s = jnp.dot(sk_ref[rows, :], qall_ref[...], preferred_element_type=F32) - bias_ref[...]
        sel = keys_ref[rows, :] >= thr
        s = jnp.where(jnp.concatenate([sel] * N_DSA_HEADS, axis=1), s, NEG_BIG)
        _softmax_update(0, s, slope_row * dist0, svT_ref[kj], m_ref, l_ref, acc_ref)
        return c

    lax.fori_loop(0, qi + 1, attend, 0)
    out = acc_ref[0] / l_ref[0]
    for h in range(N_DSA_HEADS):
        o_ref[0, hd(h), :] = out[:, lanes(h)].astype(BF16)


def _dsa_attn(fm, tok, iwT, slopes, B, S):
    nq = S // TB
    T = B * S
    topk = min(DSA_TOPK_MAX, S // 4)
    sq_blk = (2 * DIFF_W) // DSA_QW
    iq_blk = (2 * DIFF_W + DSA_QW) // IDX_QW
    sv_blk = (2 * DIFF_W + DSA_QW + IDX_QW) // DSA_HEAD_DIM
    sk_blk = DIFF_W // DSA_HEAD_DIM
    return pl.pallas_call(
        functools.partial(_dsa_attn_kernel, topk),
        grid=(B, nq),
        in_specs=[
            pl.BlockSpec(memory_space=pltpu.SMEM),
            pl.BlockSpec((1, DSA_QW, TB), lambda b, q: (b * nq + q, sq_blk, 0)),
            pl.BlockSpec((1, IDX_QW, TB), lambda b, q: (b * nq + q, iq_blk, 0)),
            pl.BlockSpec((nq, DSA_HEAD_DIM, TB), lambda b, q: (b, sv_blk, 0)),
            pl.BlockSpec((S, DSA_HEAD_DIM), lambda b, q: (b, sk_blk)),
            pl.BlockSpec((S, 2 * IDX_HEAD_DIM), lambda b, q: (b, sk_blk + 1)),
            pl.BlockSpec((1, N_IDX_HEADS, TB), lambda b, q: (b * nq + q, 0, 0)),
        ],
        out_specs=pl.BlockSpec((1, DSA_QW, TB), lambda b, q: (b * nq + q, 0, 0)),
        out_shape=jax.ShapeDtypeStruct((T // TB, DSA_QW, TB), BF16),
        scratch_shapes=[
            pltpu.VMEM((S, TB), jnp.int32),
            pltpu.VMEM((2 * IDX_HEAD_DIM, N_IDX_HEADS * TB), BF16),
            pltpu.VMEM((TB, N_DSA_HEADS * TB), F32),
            pltpu.VMEM((DSA_HEAD_DIM, N_DSA_HEADS * TB), BF16),
            pltpu.VMEM((1, 1, N_DSA_HEADS * TB), F32),
            pltpu.VMEM((1, 1, N_DSA_HEADS * TB), F32),
            pltpu.VMEM((1, DSA_HEAD_DIM, N_DSA_HEADS * TB), F32),
        ],
        compiler_params=_params(("parallel", "arbitrary")),
        name="dsa_attn",
    )(slopes, fm, fm, fm, tok, tok, iwT)


def _layer_norm_cols(y, g, b):
    mu = jnp.mean(y, axis=0, keepdims=True)
    d = y - mu
    var = jnp.mean(d * d, axis=0, keepdims=True)
    return d * lax.rsqrt(var + LN_EPS) * g + b


def _out_proj_kernel(alpha, diffT_ref, dsaT_ref, woT_ref, x_ref, g_ref, b_ref, x1T_ref, x1Tb_ref):
    nslab = diffT_ref.shape[0]
    attnT = jnp.concatenate(
        [jnp.concatenate([diffT_ref[s], dsaT_ref[s]], axis=0) for s in range(nslab)], axis=1)
    mixedT = jnp.dot(woT_ref[...], attnT, preferred_element_type=F32)
    for s in range(nslab):
        xT = x_ref[s * TB:(s + 1) * TB, :].T
        y = _layer_norm_cols(alpha * xT + mixedT[:, s * TB:(s + 1) * TB], g_ref[...], b_ref[...])
        x1T_ref[s] = y
        x1Tb_ref[s] = y.astype(BF16)


def _out_proj(diffT, dsaT, woT, x2d, g, b, alpha, tm):
    T, D = x2d.shape
    nslab = tm // TB
    fm = lambda i: (i, 0, 0)
    return pl.pallas_call(
        functools.partial(_out_proj_kernel, alpha),
        grid=(T // tm,),
        in_specs=[
            pl.BlockSpec((nslab, DIFF_W, TB), fm),
            pl.BlockSpec((nslab, DSA_QW, TB), fm),
            pl.BlockSpec((D, DIFF_W + DSA_QW), lambda i: (0, 0)),
            pl.BlockSpec((tm, D), lambda i: (i, 0)),
            pl.BlockSpec((D, 1), lambda i: (0, 0)),
            pl.BlockSpec((D, 1), lambda i: (0, 0)),
        ],
        out_specs=[pl.BlockSpec((nslab, D, TB), fm), pl.BlockSpec((nslab, D, TB), fm)],
        out_shape=[jax.ShapeDtypeStruct((T // TB, D, TB), F32), jax.ShapeDtypeStruct((T // TB, D, TB), BF16)],
        compiler_params=_params(("parallel",)),
        name="out_proj_ln1",
    )(diffT, dsaT, woT, x2d, g, b)


def _top_values(s, k):
    vals = []
    for _ in range(k):
        m = jnp.max(s, axis=0, keepdims=True)
        vals.append(m)
        s = jnp.where(s == m, -jnp.inf, s)
    return vals


def _peer_gate_kernel(x1Tb_ref, wqT_ref, k1_ref, k2_ref, c_ref, s2_ref, e1_ref, e2_ref):
    nslab = x1Tb_ref.shape[0]
    x1cat = jnp.concatenate([x1Tb_ref[s] for s in range(nslab)], axis=1)
    qT = jnp.dot(wqT_ref[...], x1cat, preferred_element_type=F32).astype(BF16)
    for h in range(PEER_HEADS):
        base = h * PEER_QDIM
        s1w = jnp.dot(k1_ref[...], qT[base:base + PEER_HALF], preferred_element_type=F32)
        s2w = jnp.dot(k2_ref[...], qT[base + PEER_HALF:base + PEER_QDIM], preferred_element_type=F32)
        for s in range(nslab):
            s1 = s1w[:, s * TB:(s + 1) * TB]
            s2 = s2w[:, s * TB:(s + 1) * TB]
            v1 = _top_values(s1, PEER_TOPK)
            v2 = _top_values(s2, PEER_TOPK)
            v2c = jnp.concatenate(v2, axis=0)
            cand = jnp.concatenate([v + v2c for v in v1], axis=0)
            tau = _top_values(cand, PEER_TOPK)[-1]
            m1, m2 = v1[0], v2[0]
            z = jnp.sum(jnp.where(cand >= tau, jnp.exp(cand - (m1 + m2)), 0.0), axis=0, keepdims=True)
            c = jnp.full(s1.shape, jnp.inf, F32)
            for vb in v2:
                c = jnp.where(s1 + vb >= tau, vb, c)
            rows = slice(h * PEER_NKEYS, (h + 1) * PEER_NKEYS)
            c_ref[s, rows, :] = c
            s2_ref[s, rows, :] = s2
            e1_ref[s, rows, :] = jnp.exp(s1 - m1) / z
            e2_ref[s, rows, :] = jnp.exp(s2 - m2)


def _peer_gate(x1Tb, wqT, k1, k2, tm):
    nblk, D, _ = x1Tb.shape
    W = PEER_HEADS * PEER_NKEYS
    nslab = tm // TB
    fm = lambda i: (i, 0, 0)
    return pl.pallas_call(
        _peer_gate_kernel,
        grid=(nblk // nslab,),
        in_specs=[
            pl.BlockSpec((nslab, D, TB), fm),
            pl.BlockSpec((PEER_HEADS * PEER_QDIM, D), lambda i: (0, 0)),
            pl.BlockSpec((PEER_NKEYS, PEER_HALF), lambda i: (0, 0)),
            pl.BlockSpec((PEER_NKEYS, PEER_HALF), lambda i: (0, 0)),
        ],
        out_specs=[pl.BlockSpec((nslab, W, TB), fm)] * 4,
        out_shape=[jax.ShapeDtypeStruct((nblk, W, TB), F32)] * 4,
        compiler_params=_params(("parallel",)),
        name="peer_gate",
    )(x1Tb, wqT, k1, k2)


def _peer_gate_units(i0, c_ref, s2_ref, e1_ref, e2_ref, g_ref):
    def unit(ii, s):
        def run():
            i = jnp.minimum(i0 + ii, PEER_NKEYS - 1)
            gate = jnp.zeros((PEER_NKEYS, TB), F32)
            for hd in range(PEER_HEADS):
                r = pl.ds(hd * PEER_NKEYS + i, 1)
                keys = slice(hd * PEER_NKEYS, (hd + 1) * PEER_NKEYS)
                sel = jnp.where(s2_ref[s, keys, :] >= c_ref[s, r, :], e2_ref[s, keys, :], 0.0)
                gate = gate + e1_ref[s, r, :] * sel
            g_ref[ii * PEER_NKEYS:(ii + 1) * PEER_NKEYS, s * TB:(s + 1) * TB] = gate
        return run
    return [unit(ii, s) for ii in range(g_ref.shape[0] // PEER_NKEYS) for s in range(c_ref.shape[0])]


def _peer_dense_kernel(x1Tb_ref, u_ref, vT_ref, c_ref, s2_ref, e1_ref, e2_ref, yT_ref,
                       ga_ref, gb_ref, h_ref, a_ref):
    ei = pl.program_id(1)
    te, tm = ga_ref.shape
    n_i = te // PEER_NKEYS
    nslab = c_ref.shape[0]
    D = u_ref.shape[1]
    kc = D // 4
    rc = D // 4
    gate_args = (c_ref, s2_ref, e1_ref, e2_ref)

    @pl.when(ei == 0)
    def _():
        yT_ref[...] = jnp.zeros_like(yT_ref)
        for unit in _peer_gate_units(0, *gate_args, ga_ref):
            unit()

    def h_unit(t, k):
        def run():
            xk = jnp.concatenate([x1Tb_ref[s, k * kc:(k + 1) * kc, :] for s in range(nslab)], axis=1)
            part = jnp.dot(u_ref[t * te:(t + 1) * te, k * kc:(k + 1) * kc], xk, preferred_element_type=F32)
            if k == 0:
                h_ref[t] = part
            else:
                h_ref[t] += part
        return run

    def act_unit(t, g_ref):
        def run():
            hT = h_ref[t]
            act = 0.5 * hT * (1.0 + lax.erf(hT * np.float32(1.0 / math.sqrt(2.0))))
            a_ref[t] = (act * g_ref[...]).astype(BF16)
        return run

    def y_unit(t, r):
        def run():
            y = jnp.dot(vT_ref[r * rc:(r + 1) * rc, t * te:(t + 1) * te], a_ref[t], preferred_element_type=F32)
            for s in range(nslab):
                yT_ref[s, r * rc:(r + 1) * rc, :] += y[:, s * TB:(s + 1) * TB]
        return run

    gb = _peer_gate_units((2 * ei + 1) * n_i, *gate_args, gb_ref)
    ga = _peer_gate_units((2 * ei + 2) * n_i, *gate_args, ga_ref)
    hA, hB = [h_unit(0, k) for k in range(4)], [h_unit(1, k) for k in range(4)]
    yA, yB = [y_unit(0, r) for r in range(4)], [y_unit(1, r) for r in range(4)]
    actA, actB = act_unit(0, ga_ref), act_unit(1, gb_ref)
    assert len(gb) == 8
    order = [hA[0], gb[0], hA[1], gb[1], hA[2], gb[2], hA[3], gb[3],
             hB[0], actA, hB[1], gb[4], hB[2], gb[5], hB[3], gb[6],
             yA[0], gb[7], yA[1], actB, yA[2], ga[0], yA[3], ga[1],
             yB[0], ga[2], ga[3], yB[1], ga[4], ga[5], yB[2], ga[6], yB[3], ga[7]]
    for unit in order:
        unit()


def _peer_dense(x1Tb, u, vT, gates, tm, te):
    nblk, D, _ = x1Tb.shape
    E = u.shape[0]
    W = PEER_HEADS * PEER_NKEYS
    nslab = tm // TB
    c, s2, e1, e2 = gates
    fm = lambda i, e: (i, 0, 0)
    return pl.pallas_call(
        _peer_dense_kernel,
        grid=(nblk // nslab, E // (2 * te)),
        in_specs=[
            pl.BlockSpec((nslab, D, TB), fm),
            pl.BlockSpec((2 * te, D), lambda i, e: (e, 0)),
            pl.BlockSpec((D, 2 * te), lambda i, e: (0, e)),
            pl.BlockSpec((nslab, W, TB), fm),
            pl.BlockSpec((nslab, W, TB), fm),
            pl.BlockSpec((nslab, W, TB), fm),
            pl.BlockSpec((nslab, W, TB), fm),
        ],
        out_specs=pl.BlockSpec((nslab, D, TB), fm),
        out_shape=jax.ShapeDtypeStruct((nblk, D, TB), F32),
        scratch_shapes=[pltpu.VMEM((te, tm), F32), pltpu.VMEM((te, tm), F32),
                        pltpu.VMEM((2, te, tm), F32), pltpu.VMEM((2, te, tm), BF16)],
        compiler_params=_params(("parallel", "arbitrary")),
        name="peer_dense",
    )(x1Tb, u, vT, c, s2, e1, e2)


def _ln2_kernel(alpha, x1T_ref, yT_ref, g_ref, b_ref, o_ref):
    for s in range(x1T_ref.shape[0]):
        z = _layer_norm_cols(alpha * x1T_ref[s] + yT_ref[s], g_ref[...], b_ref[...])
        o_ref[s * TB:(s + 1) * TB, :] = z.T


def _ln2(x1T, yT, g, b, alpha, tm):
    nblk, D, _ = x1T.shape
    nslab = tm // TB
    fm = lambda i: (i, 0, 0)
    return pl.pallas_call(
        functools.partial(_ln2_kernel, alpha),
        grid=(nblk // nslab,),
        in_specs=[
            pl.BlockSpec((nslab, D, TB), fm),
            pl.BlockSpec((nslab, D, TB), fm),
            pl.BlockSpec((D, 1), lambda i: (0, 0)),
            pl.BlockSpec((D, 1), lambda i: (0, 0)),
        ],
        out_specs=pl.BlockSpec((tm, D), lambda i: (i, 0)),
        out_shape=jax.ShapeDtypeStruct((nblk * TB, D), F32),
        compiler_params=_params(("parallel",)),
        name="ln2_out",
    )(x1T, yT, g, b)


def _tiles(T):
    tm = 2 * TB if T % (2 * TB) == 0 else TB
    return dict(proj_tm=tm, proj_tn=FM_ROWS // 3, outproj_tm=tm, dense_tm=tm, dense_te=4 * PEER_NKEYS)


def kernel(x, w_in, w_o, lambda_q1, lambda_k1, lambda_q2, lambda_k2, subln_g, ln1_g, ln1_b,
           peer_wq, peer_k1, peer_k2, peer_u, peer_v, ln2_g, ln2_b):
    B, S, D = x.shape
    T = B * S
    depth = w_in.shape[0]
    assert S % TB == 0 and w_in.shape[2] == sum(IN_SPLITS)
    alpha = float((2 * depth) ** 0.25)
    slopes_diff, slopes_dsa = _alibi_slopes()
    tiles = _tiles(T)
    offs = np.cumsum((0,) + IN_SPLITS)
    col = lambda w, k: w[:, offs[k]:offs[k + 1]]

    xt = x.reshape(T, D)
    for l in range(depth):
        w = w_in[l]
        dq, dk, dv, sq, sk, sv, iq, ik, iw = (col(w, k) for k in range(9))
        wt_fm = jnp.concatenate(
            [dq * (DIFF_QK_HALF ** -0.5), dv, sq * (DSA_HEAD_DIM ** -0.5), iq, sv], axis=1).T.astype(BF16)
        w_tok = jnp.concatenate([dk, sk, ik, ik], axis=1).astype(BF16)
        wt_iw = (iw * ((IDX_HEAD_DIM ** -0.5) * (N_IDX_HEADS ** -0.5))).T.astype(BF16)
        fm, tok, iwT = _in_proj(xt, wt_fm, w_tok, wt_iw, tiles["proj_tm"], tiles["proj_tn"])

        lam_init = 0.8 - 0.6 * math.exp(-0.3 * l)
        lam_params = jnp.stack([lambda_q1[l], lambda_k1[l], lambda_q2[l], lambda_k2[l]]).astype(F32)
        g_col = (subln_g[l].astype(F32) * (1.0 - lam_init)).reshape(DIFF_HEAD_DIM, 1)
        diffT = _diff_attn(fm, tok, slopes_diff, lam_params, g_col, lam_init, B, S)
        dsaT = _dsa_attn(fm, tok, iwT, slopes_dsa, B, S)

        colv = lambda p: p.reshape(D, 1).astype(F32)
        x1T, x1Tb = _out_proj(diffT, dsaT, w_o[l].T.astype(BF16), xt, colv(ln1_g[l]), colv(ln1_b[l]), alpha,
                              tiles["outproj_tm"])

        gates = _peer_gate(x1Tb, peer_wq[l].T.astype(BF16), peer_k1[l].astype(BF16), peer_k2[l].astype(BF16),
                           tiles["outproj_tm"])
        yT = _peer_dense(x1Tb, peer_u[l].astype(BF16), peer_v[l].T.astype(BF16), gates,
                         tiles["dense_tm"], tiles["dense_te"])
        xt = _ln2(x1T, yT, colv(ln2_g[l]), colv(ln2_b[l]), alpha, tiles["outproj_tm"])
    return xt.reshape(B, S, D)
```

```python
import functools
import math

import jax
import jax.numpy as jnp
import numpy as np
from jax import lax
from jax.experimental import pallas as pl
from jax.experimental.pallas import tpu as pltpu

F32 = jnp.float32
BF16 = jnp.bfloat16

N_DIFF_HEADS = 8
DIFF_HEAD_DIM = 128
DIFF_QK_HALF = DIFF_HEAD_DIM // 2
N_DSA_HEADS = 8
DSA_HEAD_DIM = 128
N_IDX_HEADS = 16
IDX_HEAD_DIM = 64
DSA_TOPK_MAX = 256
PEER_HEADS = 8
PEER_NKEYS = 128
PEER_QDIM = 256
PEER_HALF = PEER_QDIM // 2
PEER_TOPK = 16
LN_EPS = 1e-5
RMS_EPS = 1e-5

DIFF_W = N_DIFF_HEADS * DIFF_HEAD_DIM
DSA_QW = N_DSA_HEADS * DSA_HEAD_DIM
IDX_QW = N_IDX_HEADS * IDX_HEAD_DIM
IN_SPLITS = (DIFF_W, DIFF_W, DIFF_W, DSA_QW, DSA_HEAD_DIM, DSA_HEAD_DIM, IDX_QW, IDX_HEAD_DIM, N_IDX_HEADS)

TB = 256
LANE = 128
FM_ROWS = DIFF_W * 2 + DSA_QW + IDX_QW + DSA_HEAD_DIM
TOK_COLS = DIFF_W + DSA_HEAD_DIM + 2 * IDX_HEAD_DIM
VMEM_LIMIT = 56 * 1024 * 1024

LOG2E = math.log2(math.e)
NEG_BIG = -1e30
KEY_NEG_INF = (0xFF800000 ^ 0x7FFFFFFF) - 2 ** 32
INT_MIN = -(2 ** 31)

NT_DIMS = (((1,), (1,)), ((), ()))
TN_DIMS = (((0,), (0,)), ((), ()))


def _alibi_slopes():
    n = N_DIFF_HEADS + N_DSA_HEADS
    s = 2.0 ** (-8.0 * np.arange(1, n + 1) / n)
    return (jnp.asarray(s[0::2], dtype=F32), jnp.asarray(s[1::2], dtype=F32))


def _params(sem):
    return pltpu.CompilerParams(dimension_semantics=sem, vmem_limit_bytes=VMEM_LIMIT)


def _in_proj_kernel(x_ref, wt_ref, wtok_ref, wiw_ref, fm_ref, tok_ref, iw_ref, xb_ref):
    nslab = fm_ref.shape[0]

    @pl.when(pl.program_id(1) == 0)
    def _():
        xb = x_ref[...].astype(BF16)
        xb_ref[...] = xb
        tok_ref[...] = jnp.dot(xb, wtok_ref[...], preferred_element_type=F32).astype(BF16)
        iw = lax.dot_general(wiw_ref[...], xb, NT_DIMS, preferred_element_type=F32)
        for s in range(nslab):
            iw_ref[s] = iw[:, s * TB:(s + 1) * TB]

    r = lax.dot_general(wt_ref[...], xb_ref[...], NT_DIMS, preferred_element_type=F32)
    for s in range(nslab):
        fm_ref[s] = r[:, s * TB:(s + 1) * TB].astype(BF16)


def _in_proj(x2d, wt_fm, w_tok, wt_iw, tm, tn):
    T, D = x2d.shape
    nslab = tm // TB
    return pl.pallas_call(
        _in_proj_kernel,
        grid=(T // tm, FM_ROWS // tn),
        in_specs=[
            pl.BlockSpec((tm, D), lambda i, j: (i, 0)),
            pl.BlockSpec((tn, D), lambda i, j: (j, 0)),
            pl.BlockSpec((D, TOK_COLS), lambda i, j: (0, 0)),
            pl.BlockSpec((N_IDX_HEADS, D), lambda i, j: (0, 0)),
        ],
        out_specs=[
            pl.BlockSpec((nslab, tn, TB), lambda i, j: (i, j, 0)),
            pl.BlockSpec((tm, TOK_COLS), lambda i, j: (i, 0)),
            pl.BlockSpec((nslab, N_IDX_HEADS, TB), lambda i, j: (i, 0, 0)),
        ],
        out_shape=[
            jax.ShapeDtypeStruct((T // TB, FM_ROWS, TB), BF16),
            jax.ShapeDtypeStruct((T, TOK_COLS), BF16),
            jax.ShapeDtypeStruct((T // TB, N_IDX_HEADS, TB), F32),
        ],
        scratch_shapes=[pltpu.VMEM((tm, D), BF16)],
        compiler_params=_params(("parallel", "arbitrary")),
        name="in_proj",
    )(x2d, wt_fm, w_tok, wt_iw)


def _rel_pos():
    kk = lax.broadcasted_iota(jnp.int32, (TB, TB), 0)
    qq = lax.broadcasted_iota(jnp.int32, (TB, TB), 1)
    return (qq - kk).astype(F32)


def _softmax_reset(m_ref, l_ref, acc_ref):
    m_ref[...] = jnp.full(m_ref.shape, NEG_BIG, F32)
    l_ref[...] = jnp.zeros(l_ref.shape, F32)
    acc_ref[...] = jnp.zeros(acc_ref.shape, F32)


def _softmax_update(idx, s, shift, vt, m_ref, l_ref, acc_ref):
    m_old = m_ref[idx]
    m_new = jnp.maximum(m_old, jnp.max(s, axis=0, keepdims=True) - shift)
    alpha = jnp.exp2(m_old - m_new)
    p = jnp.exp2(s - (m_new + shift))
    l_ref[idx] = alpha * l_ref[idx] + jnp.sum(p, axis=0, keepdims=True)
    acc_ref[idx] = alpha * acc_ref[idx] + jnp.dot(vt, p.astype(BF16), preferred_element_type=F32)
    m_ref[idx] = m_new


def _diff_attn_kernel(lam_init, slopes_ref, lamp_ref, qT_ref, k_ref, vT_ref, g_ref, o_ref,
                      qm_ref, bias_ref, m_ref, l_ref, acc_ref):
    qi = pl.program_id(1)
    lp = lamp_ref[...]
    lam = (jnp.exp(jnp.sum(lp[0:1] * lp[1:2], axis=1, keepdims=True))
           - jnp.exp(jnp.sum(lp[2:3] * lp[3:4], axis=1, keepdims=True)) + lam_init)
    rel = _rel_pos()
    row = lax.broadcasted_iota(jnp.int32, (DIFF_HEAD_DIM, TB), 0)
    hd = lambda h: slice(h * DIFF_HEAD_DIM, (h + 1) * DIFF_HEAD_DIM)

    _softmax_reset(m_ref, l_ref, acc_ref)
    rel2 = jnp.concatenate([rel, rel], axis=1)
    for h in range(N_DIFF_HEADS):
        qT = qT_ref[0, hd(h), :]
        zero = jnp.zeros_like(qT)
        qm_ref[h] = jnp.concatenate(
            [jnp.where(row < DIFF_QK_HALF, qT, zero), jnp.where(row >= DIFF_QK_HALF, qT, zero)], axis=1)
        bias_ref[h] = slopes_ref[h] * rel2

    def step(kj, diag):
        rows = pl.ds(pl.multiple_of(kj * TB, TB), TB)
        dist0 = ((qi - kj) * TB).astype(F32)
        for h in range(N_DIFF_HEADS):
            s = jnp.dot(k_ref[rows, hd(h)], qm_ref[h], preferred_element_type=F32) - bias_ref[h]
            if diag:
                s = jnp.where(rel2 >= 0, s, NEG_BIG)
            _softmax_update(h, s, slopes_ref[h] * dist0, vT_ref[kj, hd(h), :], m_ref, l_ref, acc_ref)

    def body(kj, c):
        step(kj, False)
        return c

    lax.fori_loop(0, qi, body, 0)
    step(qi, True)
    for h in range(N_DIFF_HEADS):
        w = acc_ref[h] / l_ref[h]
        out = w[:, :TB] - lam * w[:, TB:]
        ms = jnp.mean(out * out, axis=0, keepdims=True)
        o_ref[0, hd(h), :] = (out * lax.rsqrt(ms + RMS_EPS) * g_ref[...]).astype(BF16)


def _diff_attn(fm, tok, slopes, lam_params, g_col, lam_init, B, S):
    nq = S // TB
    T = B * S
    nchain = 2 * N_DIFF_HEADS
    return pl.pallas_call(
        functools.partial(_diff_attn_kernel, lam_init),
        grid=(B, nq),
        in_specs=[
            pl.BlockSpec(memory_space=pltpu.SMEM),
            pl.BlockSpec((4, DIFF_QK_HALF), lambda b, q: (0, 0)),
            pl.BlockSpec((1, DIFF_W, TB), lambda b, q: (b * nq + q, 0, 0)),
            pl.BlockSpec((S, DIFF_W), lambda b, q: (b, 0)),
            pl.BlockSpec((nq, DIFF_W, TB), lambda b, q: (b, 1, 0)),
            pl.BlockSpec((DIFF_HEAD_DIM, 1), lambda b, q: (0, 0)),
        ],
        out_specs=pl.BlockSpec((1, DIFF_W, TB), lambda b, q: (b * nq + q, 0, 0)),
        out_shape=jax.ShapeDtypeStruct((T // TB, DIFF_W, TB), BF16),
        scratch_shapes=[
            pltpu.VMEM((N_DIFF_HEADS, DIFF_HEAD_DIM, 2 * TB), BF16),
            pltpu.VMEM((N_DIFF_HEADS, TB, 2 * TB), F32),
            pltpu.VMEM((N_DIFF_HEADS, 1, 2 * TB), F32),
            pltpu.VMEM((N_DIFF_HEADS, 1, 2 * TB), F32),
            pltpu.VMEM((N_DIFF_HEADS, DIFF_HEAD_DIM, 2 * TB), F32),
        ],
        compiler_params=_params(("parallel", "arbitrary")),
        name="diff_attn",
    )(slopes, lam_params, fm, tok, fm, g_col)


def _dsa_attn_kernel(topk, slopes_ref, sqT_ref, iqT_ref, svT_ref, sk_ref, ikk_ref, iwT_ref, o_ref, keys_ref,
                     iqall_ref, bias_ref, qall_ref, m_ref, l_ref, acc_ref):
    qi = pl.program_id(1)
    rel = _rel_pos()
    iw = iwT_ref[0]
    row = lax.broadcasted_iota(jnp.int32, (2 * IDX_HEAD_DIM, TB), 0)

    for p in range(N_IDX_HEADS // 2):
        pair = iqT_ref[0, p * 2 * IDX_HEAD_DIM:(p + 1) * 2 * IDX_HEAD_DIM, :]
        zero = jnp.zeros_like(pair)
        iqall_ref[:, (2 * p) * TB:(2 * p + 1) * TB] = jnp.where(row < IDX_HEAD_DIM, pair, zero)
        iqall_ref[:, (2 * p + 1) * TB:(2 * p + 2) * TB] = jnp.where(row >= IDX_HEAD_DIM, pair, zero)

    def score_block(kj, diag):
        rows = pl.ds(pl.multiple_of(kj * TB, TB), TB)
        ikk = ikk_ref[rows, :]
        acc = jnp.zeros((TB, TB), F32)
        half = N_IDX_HEADS // 2
        for g in range(2):
            r = jnp.dot(ikk, iqall_ref[:, g * half * TB:(g + 1) * half * TB], preferred_element_type=F32)
            for j in range(half):
                hh = g * half + j
                acc = acc + iw[hh:hh + 1, :] * jnp.maximum(r[:, j * TB:(j + 1) * TB], 0.0)
        if diag:
            acc = jnp.where(rel >= 0, acc, -jnp.inf)
        bits = pltpu.bitcast(acc, jnp.int32)
        keys_ref[rows, :] = jnp.where(bits >= 0, bits, bits ^ jnp.int32(0x7FFFFFFF))

    def score_body(kj, c):
        score_block(kj, False)
        return c

    lax.fori_loop(0, qi, score_body, 0)
    score_block(qi, True)

    def count_ge(cand):
        def body(kj, cnt):
            blk = keys_ref[pl.ds(pl.multiple_of(kj * TB, TB), TB), :]
            return cnt + jnp.sum(jnp.where(blk >= cand, 1, 0).astype(jnp.int32), axis=0, keepdims=True)
        return lax.fori_loop(0, qi + 1, body, jnp.zeros((1, TB), jnp.int32))

    lo = jnp.where(count_ge(jnp.zeros((1, TB), jnp.int32)) >= topk, jnp.int32(0), jnp.int32(INT_MIN))

    def bit_body(t, lo):
        cand = lo + lax.shift_left(jnp.int32(1), jnp.int32(30) - t)
        return jnp.where(count_ge(cand) >= topk, cand, lo)

    lo = lax.fori_loop(0, 31, bit_body, lo)
    thr = jnp.maximum(lo, jnp.int32(KEY_NEG_INF + 1))

    hd = lambda h: slice(h * DSA_HEAD_DIM, (h + 1) * DSA_HEAD_DIM)
    lanes = lambda h: slice(h * TB, (h + 1) * TB)
    _softmax_reset(m_ref, l_ref, acc_ref)
    for h in range(N_DSA_HEADS):
        bias_ref[:, lanes(h)] = slopes_ref[h] * rel
        qall_ref[:, lanes(h)] = sqT_ref[0, hd(h), :]
    slope_row = jnp.concatenate([jnp.full((1, TB), slopes_ref[h], F32) for h in range(N_DSA_HEADS)], axis=1)

    def attend(kj, c):
        rows = pl.ds(pl.multiple_of(kj * TB, TB), TB)
        dist0 = ((qi - kj) * TB).astype(F32)
        s = jnp.dot(sk_ref[rows, :], qall_ref[...], preferred_element_type=F32) - bias_ref[...]
        sel = keys_ref[rows, :] >= thr
        s = jnp.where(jnp.concatenate([sel] * N_DSA_HEADS, axis=1), s, NEG_BIG)
        _softmax_update(0, s, slope_row * dist0, svT_ref[kj], m_ref, l_ref, acc_ref)
        return c

    lax.fori_loop(0, qi + 1, attend, 0)
    out = acc_ref[0] / l_ref[0]
    for h in range(N_DSA_HEADS):
        o_ref[0, hd(h), :] = out[:, lanes(h)].astype(BF16)


def _dsa_attn(fm, tok, iwT, slopes, B, S):
    nq = S // TB
    T = B * S
    topk = min(DSA_TOPK_MAX, S // 4)
    sq_blk = (2 * DIFF_W) // DSA_QW
    iq_blk = (2 * DIFF_W + DSA_QW) // IDX_QW
    sv_blk = (2 * DIFF_W + DSA_QW + IDX_QW) // DSA_HEAD_DIM
    sk_blk = DIFF_W // DSA_HEAD_DIM
    return pl.pallas_call(
        functools.partial(_dsa_attn_kernel, topk),
        grid=(B, nq),
        in_specs=[
            pl.BlockSpec(memory_space=pltpu.SMEM),
            pl.BlockSpec((1, DSA_QW, TB), lambda b, q: (b * nq + q, sq_blk, 0)),
            pl.BlockSpec((1, IDX_QW, TB), lambda b, q: (b * nq + q, iq_blk, 0)),
            pl.BlockSpec((nq, DSA_HEAD_DIM, TB), lambda b, q: (b, sv_blk, 0)),
            pl.BlockSpec((S, DSA_HEAD_DIM), lambda b, q: (b, sk_blk)),
            pl.BlockSpec((S, 2 * IDX_HEAD_DIM), lambda b, q: (b, sk_blk + 1)),
            pl.BlockSpec((1, N_IDX_HEADS, TB), lambda b, q: (b * nq + q, 0, 0)),
        ],
        out_specs=pl.BlockSpec((1, DSA_QW, TB), lambda b, q: (b * nq + q, 0, 0)),
        out_shape=jax.ShapeDtypeStruct((T // TB, DSA_QW, TB), BF16),
        scratch_shapes=[
            pltpu.VMEM((S, TB), jnp.int32),
            pltpu.VMEM((2 * IDX_HEAD_DIM, N_IDX_HEADS * TB), BF16),
            pltpu.VMEM((TB, N_DSA_HEADS * TB), F32),
            pltpu.VMEM((DSA_HEAD_DIM, N_DSA_HEADS * TB), BF16),
            pltpu.VMEM((1, 1, N_DSA_HEADS * TB), F32),
            pltpu.VMEM((1, 1, N_DSA_HEADS * TB), F32),
            pltpu.VMEM((1, DSA_HEAD_DIM, N_DSA_HEADS * TB), F32),
        ],
        compiler_params=_params(("parallel", "arbitrary")),
        name="dsa_attn",
    )(slopes, fm, fm, fm, tok, tok, iwT)


def _layer_norm_cols(y, g, b):
    mu = jnp.mean(y, axis=0, keepdims=True)
    d = y - mu
    var = jnp.mean(d * d, axis=0, keepdims=True)
    return d * lax.rsqrt(var + LN_EPS) * g + b


def _out_proj_kernel(alpha, diffT_ref, dsaT_ref, woT_ref, x_ref, g_ref, b_ref, x1T_ref, x1Tb_ref):
    nslab = diffT_ref.shape[0]
    attnT = jnp.concatenate(
        [jnp.concatenate([diffT_ref[s], dsaT_ref[s]], axis=0) for s in range(nslab)], axis=1)
    mixedT = jnp.dot(woT_ref[...], attnT, preferred_element_type=F32)
    for s in range(nslab):
        xT = x_ref[s * TB:(s + 1) * TB, :].T
        y = _layer_norm_cols(alpha * xT + mixedT[:, s * TB:(s + 1) * TB], g_ref[...], b_ref[...])
        x1T_ref[s] = y
        x1Tb_ref[s] = y.astype(BF16)


def _out_proj(diffT, dsaT, woT, x2d, g, b, alpha, tm):
    T, D = x2d.shape
    nslab = tm // TB
    fm = lambda i: (i, 0, 0)
    return pl.pallas_call(
        functools.partial(_out_proj_kernel, alpha),
        grid=(T // tm,),
        in_specs=[
            pl.BlockSpec((nslab, DIFF_W, TB), fm),
            pl.BlockSpec((nslab, DSA_QW, TB), fm),
            pl.BlockSpec((D, DIFF_W + DSA_QW), lambda i: (0, 0)),
            pl.BlockSpec((tm, D), lambda i: (i, 0)),
            pl.BlockSpec((D, 1), lambda i: (0, 0)),
            pl.BlockSpec((D, 1), lambda i: (0, 0)),
        ],
        out_specs=[pl.BlockSpec((nslab, D, TB), fm), pl.BlockSpec((nslab, D, TB), fm)],
        out_shape=[jax.ShapeDtypeStruct((T // TB, D, TB), F32), jax.ShapeDtypeStruct((T // TB, D, TB), BF16)],
        compiler_params=_params(("parallel",)),
        name="out_proj_ln1",
    )(diffT, dsaT, woT, x2d, g, b)


def _top_values(s, k):
    vals = []
    for _ in range(k):
        m = jnp.max(s, axis=0, keepdims=True)
        vals.append(m)
        s = jnp.where(s == m, -jnp.inf, s)
    return vals


def _peer_gate_kernel(x1Tb_ref, wqT_ref, k1_ref, k2_ref, c_ref, s2_ref, e1_ref, e2_ref):
    nslab = x1Tb_ref.shape[0]
    x1cat = jnp.concatenate([x1Tb_ref[s] for s in range(nslab)], axis=1)
    qT = jnp.dot(wqT_ref[...], x1cat, preferred_element_type=F32).astype(BF16)
    for h in range(PEER_HEADS):
        base = h * PEER_QDIM
        s1w = jnp.dot(k1_ref[...], qT[base:base + PEER_HALF], preferred_element_type=F32)
        s2w = jnp.dot(k2_ref[...], qT[base + PEER_HALF:base + PEER_QDIM], preferred_element_type=F32)
        for s in range(nslab):
            s1 = s1w[:, s * TB:(s + 1) * TB]
            s2 = s2w[:, s * TB:(s + 1) * TB]
            v1 = _top_values(s1, PEER_TOPK)
            v2 = _top_values(s2, PEER_TOPK)
            v1c = jnp.concatenate(v1, axis=0)
            v2c = jnp.concatenate(v2, axis=0)
            half = PEER_TOPK // 2
            cand = jnp.concatenate(
                [v1[0] + v2c] + [v1[a] + v2c[:half] for a in range(1, half)] + [v1c[half:] + v2[0]], axis=0)
            tau = _top_values(cand, PEER_TOPK)[-1]
            m1, m2 = v1[0], v2[0]
            z = jnp.sum(jnp.where(cand >= tau, jnp.exp(cand - (m1 + m2)), 0.0), axis=0, keepdims=True)
            c = jnp.full(s1.shape, jnp.inf, F32)
            for vb in v2:
                c = jnp.where(s1 + vb >= tau, vb, c)
            rows = slice(h * PEER_NKEYS, (h + 1) * PEER_NKEYS)
            c_ref[s, rows, :] = c
            s2_ref[s, rows, :] = s2
            e1_ref[s, rows, :] = jnp.exp(s1 - m1) / z
            e2_ref[s, rows, :] = jnp.exp(s2 - m2)


def _peer_gate(x1Tb, wqT, k1, k2, tm):
    nblk, D, _ = x1Tb.shape
    W = PEER_HEADS * PEER_NKEYS
    nslab = tm // TB
    fm = lambda i: (i, 0, 0)
    return pl.pallas_call(
        _peer_gate_kernel,
        grid=(nblk // nslab,),
        in_specs=[
            pl.BlockSpec((nslab, D, TB), fm),
            pl.BlockSpec((PEER_HEADS * PEER_QDIM, D), lambda i: (0, 0)),
            pl.BlockSpec((PEER_NKEYS, PEER_HALF), lambda i: (0, 0)),
            pl.BlockSpec((PEER_NKEYS, PEER_HALF), lambda i: (0, 0)),
        ],
        out_specs=[pl.BlockSpec((nslab, W, TB), fm)] * 4,
        out_shape=[jax.ShapeDtypeStruct((nblk, W, TB), F32)] * 4,
        compiler_params=_params(("parallel",)),
        name="peer_gate",
    )(x1Tb, wqT, k1, k2)


def _peer_gate_units(i0, c_ref, s2_ref, e1_ref, e2_ref, g_ref):
    def unit(ii, s):
        def run():
            i = jnp.minimum(i0 + ii, PEER_NKEYS - 1)
            gate = jnp.zeros((PEER_NKEYS, TB), F32)
            for hd in range(PEER_HEADS):
                r = pl.ds(hd * PEER_NKEYS + i, 1)
                keys = slice(hd * PEER_NKEYS, (hd + 1) * PEER_NKEYS)
                sel = jnp.where(s2_ref[s, keys, :] >= c_ref[s, r, :], e2_ref[s, keys, :], 0.0)
                gate = gate + e1_ref[s, r, :] * sel
            g_ref[ii * PEER_NKEYS:(ii + 1) * PEER_NKEYS, s * TB:(s + 1) * TB] = gate
        return run
    return [unit(ii, s) for ii in range(g_ref.shape[0] // PEER_NKEYS) for s in range(c_ref.shape[0])]


def _peer_dense_kernel(x1Tb_ref, u_ref, vT_ref, c_ref, s2_ref, e1_ref, e2_ref, yT_ref,
                       ga_ref, gb_ref, h_ref, a_ref):
    ei = pl.program_id(1)
    te, tm = ga_ref.shape
    n_i = te // PEER_NKEYS
    nslab = c_ref.shape[0]
    D = u_ref.shape[1]
    kc = D // 4
    rc = D // 4
    gate_args = (c_ref, s2_ref, e1_ref, e2_ref)

    @pl.when(ei == 0)
    def _():
        yT_ref[...] = jnp.zeros_like(yT_ref)
        for unit in _peer_gate_units(0, *gate_args, ga_ref):
            unit()

    def h_unit(t, k):
        def run():
            xk = jnp.concatenate([x1Tb_ref[s, k * kc:(k + 1) * kc, :] for s in range(nslab)], axis=1)
            part = jnp.dot(u_ref[t * te:(t + 1) * te, k * kc:(k + 1) * kc], xk, preferred_element_type=F32)
            if k == 0:
                h_ref[t] = part
            else:
                h_ref[t] += part
        return run

    def act_unit(t, g_ref):
        def run():
            hT = h_ref[t]
            act = 0.5 * hT * (1.0 + lax.erf(hT * np.float32(1.0 / math.sqrt(2.0))))
            a_ref[t] = (act * g_ref[...]).astype(BF16)
        return run

    def y_unit(t, r):
        def run():
            y = jnp.dot(vT_ref[r * rc:(r + 1) * rc, t * te:(t + 1) * te], a_ref[t], preferred_element_type=F32)
            for s in range(nslab):
                yT_ref[s, r * rc:(r + 1) * rc, :] += y[:, s * TB:(s + 1) * TB]
        return run

    gb = _peer_gate_units((2 * ei + 1) * n_i, *gate_args, gb_ref)
    ga = _peer_gate_units((2 * ei + 2) * n_i, *gate_args, ga_ref)
    hA, hB = [h_unit(0, k) for k in range(4)], [h_unit(1, k) for k in range(4)]
    yA, yB = [y_unit(0, r) for r in range(4)], [y_unit(1, r) for r in range(4)]
    actA, actB = act_unit(0, ga_ref), act_unit(1, gb_ref)
    assert len(gb) == 8
    order = [hA[0], gb[0], hA[1], gb[1], hA[2], gb[2], hA[3], gb[3],
             hB[0], actA, hB[1], gb[4], hB[2], gb[5], hB[3], gb[6],
             yA[0], gb[7], yA[1], actB, yA[2], ga[0], yA[3], ga[1],
             yB[0], ga[2], ga[3], yB[1], ga[4], ga[5], yB[2], ga[6], yB[3], ga[7]]
    for unit in order:
        unit()


def _peer_dense(x1Tb, u, vT, gates, tm, te):
    nblk, D, _ = x1Tb.shape
    E = u.shape[0]
    W = PEER_HEADS * PEER_NKEYS
    nslab = tm // TB
    c, s2, e1, e2 = gates
    fm = lambda i, e: (i, 0, 0)
    return pl.pallas_call(
        _peer_dense_kernel,
        grid=(nblk // nslab, E // (2 * te)),
        in_specs=[
            pl.BlockSpec((nslab, D, TB), fm),
            pl.BlockSpec((2 * te, D), lambda i, e: (e, 0)),
            pl.BlockSpec((D, 2 * te), lambda i, e: (0, e)),
            pl.BlockSpec((nslab, W, TB), fm),
            pl.BlockSpec((nslab, W, TB), fm),
            pl.BlockSpec((nslab, W, TB), fm),
            pl.BlockSpec((nslab, W, TB), fm),
        ],
        out_specs=pl.BlockSpec((nslab, D, TB), fm),
        out_shape=jax.ShapeDtypeStruct((nblk, D, TB), F32),
        scratch_shapes=[pltpu.VMEM((te, tm), F32), pltpu.VMEM((te, tm), F32),
                        pltpu.VMEM((2, te, tm), F32), pltpu.VMEM((2, te, tm), BF16)],
        compiler_params=_params(("parallel", "arbitrary")),
        name="peer_dense",
    )(x1Tb, u, vT, c, s2, e1, e2)


def _ln2_kernel(alpha, x1T_ref, yT_ref, g_ref, b_ref, o_ref):
    for s in range(x1T_ref.shape[0]):
        z = _layer_norm_cols(alpha * x1T_ref[s] + yT_ref[s], g_ref[...], b_ref[...])
        o_ref[s * TB:(s + 1) * TB, :] = z.T


def _ln2(x1T, yT, g, b, alpha, tm):
    nblk, D, _ = x1T.shape
    nslab = tm // TB
    fm = lambda i: (i, 0, 0)
    return pl.pallas_call(
        functools.partial(_ln2_kernel, alpha),
        grid=(nblk // nslab,),
        in_specs=[
            pl.BlockSpec((nslab, D, TB), fm),
            pl.BlockSpec((nslab, D, TB), fm),
            pl.BlockSpec((D, 1), lambda i: (0, 0)),
            pl.BlockSpec((D, 1), lambda i: (0, 0)),
        ],
        out_specs=pl.BlockSpec((tm, D), lambda i: (i, 0)),
        out_shape=jax.ShapeDtypeStruct((nblk * TB, D), F32),
        compiler_params=_params(("parallel",)),
        name="ln2_out",
    )(x1T, yT, g, b)


def _tiles(T):
    tm = 2 * TB if T % (2 * TB) == 0 else TB
    return dict(proj_tm=tm, proj_tn=FM_ROWS // 3, outproj_tm=tm, dense_tm=tm, dense_te=4 * PEER_NKEYS)


def kernel(x, w_in, w_o, lambda_q1, lambda_k1, lambda_q2, lambda_k2, subln_g, ln1_g, ln1_b,
           peer_wq, peer_k1, peer_k2, peer_u, peer_v, ln2_g, ln2_b):
    B, S, D = x.shape
    T = B * S
    depth = w_in.shape[0]
    assert S % TB == 0 and w_in.shape[2] == sum(IN_SPLITS)
    alpha = float((2 * depth) ** 0.25)
    slopes_diff, slopes_dsa = (s * F32(LOG2E) for s in _alibi_slopes())
    tiles = _tiles(T)
    offs = np.cumsum((0,) + IN_SPLITS)
    col = lambda w, k: w[:, offs[k]:offs[k + 1]]

    xt = x.reshape(T, D)
    for l in range(depth):
        w = w_in[l]
        dq, dk, dv, sq, sk, sv, iq, ik, iw = (col(w, k) for k in range(9))
        wt_fm = jnp.concatenate(
            [dq * (LOG2E * DIFF_QK_HALF ** -0.5), dv, sq * (LOG2E * DSA_HEAD_DIM ** -0.5), iq, sv],
            axis=1).T.astype(BF16)
        w_tok = jnp.concatenate([dk, sk, ik, ik], axis=1).astype(BF16)
        wt_iw = (iw * ((IDX_HEAD_DIM ** -0.5) * (N_IDX_HEADS ** -0.5))).T.astype(BF16)
        fm, tok, iwT = _in_proj(xt, wt_fm, w_tok, wt_iw, tiles["proj_tm"], tiles["proj_tn"])

        lam_init = 0.8 - 0.6 * math.exp(-0.3 * l)
        lam_params = jnp.stack([lambda_q1[l], lambda_k1[l], lambda_q2[l], lambda_k2[l]]).astype(F32)
        g_col = (subln_g[l].astype(F32) * (1.0 - lam_init)).reshape(DIFF_HEAD_DIM, 1)
        diffT = _diff_attn(fm, tok, slopes_diff, lam_params, g_col, lam_init, B, S)
        dsaT = _dsa_attn(fm, tok, iwT, slopes_dsa, B, S)

        colv = lambda p: p.reshape(D, 1).astype(F32)
        x1T, x1Tb = _out_proj(diffT, dsaT, w_o[l].T.astype(BF16), xt, colv(ln1_g[l]), colv(ln1_b[l]), alpha,
                              tiles["outproj_tm"])

        gates = _peer_gate(x1Tb, peer_wq[l].T.astype(BF16), peer_k1[l].astype(BF16), peer_k2[l].astype(BF16),
                           tiles["outproj_tm"])
        yT = _peer_dense(x1Tb, peer_u[l].astype(BF16), peer_v[l].T.astype(BF16), gates,
                         tiles["dense_tm"], tiles["dense_te"])
        xt = _ln2(x1T, yT, colv(ln2_g[l]), colv(ln2_b[l]), alpha, tiles["outproj_tm"])
    return xt.reshape(B, S, D)
```

```python
import functools
import math

import jax
import jax.numpy as jnp
import numpy as np
from jax import lax
from jax.experimental import pallas as pl
from jax.experimental.pallas import tpu as pltpu

F32 = jnp.float32
BF16 = jnp.bfloat16

N_DIFF_HEADS = 8
DIFF_HEAD_DIM = 128
DIFF_QK_HALF = DIFF_HEAD_DIM // 2
N_DSA_HEADS = 8
DSA_HEAD_DIM = 128
N_IDX_HEADS = 16
IDX_HEAD_DIM = 64
DSA_TOPK_MAX = 256
PEER_HEADS = 8
PEER_NKEYS = 128
PEER_QDIM = 256
PEER_HALF = PEER_QDIM // 2
PEER_TOPK = 16
LN_EPS = 1e-5
RMS_EPS = 1e-5

DIFF_W = N_DIFF_HEADS * DIFF_HEAD_DIM
DSA_QW = N_DSA_HEADS * DSA_HEAD_DIM
IDX_QW = N_IDX_HEADS * IDX_HEAD_DIM
IN_SPLITS = (DIFF_W, DIFF_W, DIFF_W, DSA_QW, DSA_HEAD_DIM, DSA_HEAD_DIM, IDX_QW, IDX_HEAD_DIM, N_IDX_HEADS)

TB = 256
LANE = 128
FM_ROWS = DIFF_W * 2 + DSA_QW + IDX_QW + DSA_HEAD_DIM
TOK_COLS = DIFF_W + DSA_HEAD_DIM + 2 * IDX_HEAD_DIM
VMEM_LIMIT = 56 * 1024 * 1024

LOG2E = math.log2(math.e)
NEG_BIG = -1e30
KEY_NEG_INF = (0xFF800000 ^ 0x7FFFFFFF) - 2 ** 32
INT_MIN = -(2 ** 31)

NT_DIMS = (((1,), (1,)), ((), ()))
TN_DIMS = (((0,), (0,)), ((), ()))


def _alibi_slopes():
    n = N_DIFF_HEADS + N_DSA_HEADS
    s = 2.0 ** (-8.0 * np.arange(1, n + 1) / n)
    return (jnp.asarray(s[0::2], dtype=F32), jnp.asarray(s[1::2], dtype=F32))


def _params(sem):
    return pltpu.CompilerParams(dimension_semantics=sem, vmem_limit_bytes=VMEM_LIMIT)


def _in_proj_kernel(x_ref, wt_ref, wtok_ref, wiw_ref, fm_ref, tok_ref, iw_ref, xb_ref):
    nslab = fm_ref.shape[0]

    @pl.when(pl.program_id(1) == 0)
    def _():
        xb = x_ref[...].astype(BF16)
        xb_ref[...] = xb
        tok_ref[...] = jnp.dot(xb, wtok_ref[...], preferred_element_type=F32).astype(BF16)
        iw = lax.dot_general(wiw_ref[...], xb, NT_DIMS, preferred_element_type=F32)
        for s in range(nslab):
            iw_ref[s] = iw[:, s * TB:(s + 1) * TB]

    r = lax.dot_general(wt_ref[...], xb_ref[...], NT_DIMS, preferred_element_type=F32)
    for s in range(nslab):
        fm_ref[s] = r[:, s * TB:(s + 1) * TB].astype(BF16)


def _in_proj(x2d, wt_fm, w_tok, wt_iw, tm, tn):
    T, D = x2d.shape
    nslab = tm // TB
    return pl.pallas_call(
        _in_proj_kernel,
        grid=(T // tm, FM_ROWS // tn),
        in_specs=[
            pl.BlockSpec((tm, D), lambda i, j: (i, 0)),
            pl.BlockSpec((tn, D), lambda i, j: (j, 0)),
            pl.BlockSpec((D, TOK_COLS), lambda i, j: (0, 0)),
            pl.BlockSpec((N_IDX_HEADS, D), lambda i, j: (0, 0)),
        ],
        out_specs=[
            pl.BlockSpec((nslab, tn, TB), lambda i, j: (i, j, 0)),
            pl.BlockSpec((tm, TOK_COLS), lambda i, j: (i, 0)),
            pl.BlockSpec((nslab, N_IDX_HEADS, TB), lambda i, j: (i, 0, 0)),
        ],
        out_shape=[
            jax.ShapeDtypeStruct((T // TB, FM_ROWS, TB), BF16),
            jax.ShapeDtypeStruct((T, TOK_COLS), BF16),
            jax.ShapeDtypeStruct((T // TB, N_IDX_HEADS, TB), F32),
        ],
        scratch_shapes=[pltpu.VMEM((tm, D), BF16)],
        compiler_params=_params(("parallel", "arbitrary")),
        name="in_proj",
    )(x2d, wt_fm, w_tok, wt_iw)


def _rel_pos():
    kk = lax.broadcasted_iota(jnp.int32, (TB, TB), 0)
    qq = lax.broadcasted_iota(jnp.int32, (TB, TB), 1)
    return (qq - kk).astype(F32)


def _softmax_reset(m_ref, l_ref, acc_ref):
    m_ref[...] = jnp.full(m_ref.shape, NEG_BIG, F32)
    l_ref[...] = jnp.zeros(l_ref.shape, F32)
    acc_ref[...] = jnp.zeros(acc_ref.shape, F32)


def _softmax_update(idx, s, shift, vt, m_ref, l_ref, acc_ref):
    m_old = m_ref[idx]
    m_new = jnp.maximum(m_old, jnp.max(s, axis=0, keepdims=True) - shift)
    alpha = jnp.exp2(m_old - m_new)
    p = jnp.exp2(s - (m_new + shift))
    l_ref[idx] = alpha * l_ref[idx] + jnp.sum(p, axis=0, keepdims=True)
    acc_ref[idx] = alpha * acc_ref[idx] + jnp.dot(vt, p.astype(BF16), preferred_element_type=F32)
    m_ref[idx] = m_new


def _diff_attn_kernel(lam_init, slopes_ref, lamp_ref, qT_ref, k_ref, vT_ref, g_ref, o_ref,
                      qm_ref, bias_ref, m_ref, l_ref, acc_ref):
    qi = pl.program_id(1)
    lp = lamp_ref[...]
    lam = (jnp.exp(jnp.sum(lp[0:1] * lp[1:2], axis=1, keepdims=True))
           - jnp.exp(jnp.sum(lp[2:3] * lp[3:4], axis=1, keepdims=True)) + lam_init)
    rel = _rel_pos()
    row = lax.broadcasted_iota(jnp.int32, (DIFF_HEAD_DIM, TB), 0)
    hd = lambda h: slice(h * DIFF_HEAD_DIM, (h + 1) * DIFF_HEAD_DIM)

    _softmax_reset(m_ref, l_ref, acc_ref)
    rel2 = jnp.concatenate([rel, rel], axis=1)
    for h in range(N_DIFF_HEADS):
        qT = qT_ref[0, hd(h), :]
        zero = jnp.zeros_like(qT)
        qm_ref[h] = jnp.concatenate(
            [jnp.where(row < DIFF_QK_HALF, qT, zero), jnp.where(row >= DIFF_QK_HALF, qT, zero)], axis=1)
        bias_ref[h] = slopes_ref[h] * rel2

    def step(kj, diag):
        rows = pl.ds(pl.multiple_of(kj * TB, TB), TB)
        dist0 = ((qi - kj) * TB).astype(F32)
        for h in range(N_DIFF_HEADS):
            s = jnp.dot(k_ref[rows, hd(h)], qm_ref[h], preferred_element_type=F32) - bias_ref[h]
            if diag:
                s = jnp.where(rel2 >= 0, s, NEG_BIG)
            _softmax_update(h, s, slopes_ref[h] * dist0, vT_ref[kj, hd(h), :], m_ref, l_ref, acc_ref)

    def body(kj, c):
        step(kj, False)
        return c

    lax.fori_loop(0, qi, body, 0)
    step(qi, True)
    for h in range(N_DIFF_HEADS):
        w = acc_ref[h] / l_ref[h]
        out = w[:, :TB] - lam * w[:, TB:]
        ms = jnp.mean(out * out, axis=0, keepdims=True)
        o_ref[0, hd(h), :] = (out * lax.rsqrt(ms + RMS_EPS) * g_ref[...]).astype(BF16)


def _diff_attn(fm, tok, slopes, lam_params, g_col, lam_init, B, S):
    nq = S // TB
    T = B * S
    nchain = 2 * N_DIFF_HEADS
    return pl.pallas_call(
        functools.partial(_diff_attn_kernel, lam_init),
        grid=(B, nq),
        in_specs=[
            pl.BlockSpec(memory_space=pltpu.SMEM),
            pl.BlockSpec((4, DIFF_QK_HALF), lambda b, q: (0, 0)),
            pl.BlockSpec((1, DIFF_W, TB), lambda b, q: (b * nq + q, 0, 0)),
            pl.BlockSpec((S, DIFF_W), lambda b, q: (b, 0)),
            pl.BlockSpec((nq, DIFF_W, TB), lambda b, q: (b, 1, 0)),
            pl.BlockSpec((DIFF_HEAD_DIM, 1), lambda b, q: (0, 0)),
        ],
        out_specs=pl.BlockSpec((1, DIFF_W, TB), lambda b, q: (b * nq + q, 0, 0)),
        out_shape=jax.ShapeDtypeStruct((T // TB, DIFF_W, TB), BF16),
        scratch_shapes=[
            pltpu.VMEM((N_DIFF_HEADS, DIFF_HEAD_DIM, 2 * TB), BF16),
            pltpu.VMEM((N_DIFF_HEADS, TB, 2 * TB), F32),
            pltpu.VMEM((N_DIFF_HEADS, 1, 2 * TB), F32),
            pltpu.VMEM((N_DIFF_HEADS, 1, 2 * TB), F32),
            pltpu.VMEM((N_DIFF_HEADS, DIFF_HEAD_DIM, 2 * TB), F32),
        ],
        compiler_params=_params(("parallel", "arbitrary")),
        name="diff_attn",
    )(slopes, lam_params, fm, tok, fm, g_col)


def _dsa_attn_kernel(topk, slopes_ref, sqT_ref, iqT_ref, svT_ref, sk_ref, ikk_ref, iwT_ref, o_ref, keys_ref,
                     iqall_ref, bias_ref, qall_ref, m_ref, l_ref, acc_ref):
    qi = pl.program_id(1)
    rel = _rel_pos()
    iw = iwT_ref[0]
    row = lax.broadcasted_iota(jnp.int32, (2 * IDX_HEAD_DIM, TB), 0)

    for p in range(N_IDX_HEADS // 2):
        pair = iqT_ref[0, p * 2 * IDX_HEAD_DIM:(p + 1) * 2 * IDX_HEAD_DIM, :]
        zero = jnp.zeros_like(pair)
        iqall_ref[:, (2 * p) * TB:(2 * p + 1) * TB] = jnp.where(row < IDX_HEAD_DIM, pair, zero)
        iqall_ref[:, (2 * p + 1) * TB:(2 * p + 2) * TB] = jnp.where(row >= IDX_HEAD_DIM, pair, zero)

    def score_block(kj, diag):
        rows = pl.ds(pl.multiple_of(kj * TB, TB), TB)
        ikk = ikk_ref[rows, :]
        acc = jnp.zeros((TB, TB), F32)
        half = N_IDX_HEADS // 2
        for g in range(2):
            r = jnp.dot(ikk, iqall_ref[:, g * half * TB:(g + 1) * half * TB], preferred_element_type=F32)
            for j in range(half):
                hh = g * half + j
                acc = acc + iw[hh:hh + 1, :] * jnp.maximum(r[:, j * TB:(j + 1) * TB], 0.0)
        if diag:
            acc = jnp.where(rel >= 0, acc, -jnp.inf)
        bits = pltpu.bitcast(acc, jnp.int32)
        keys_ref[rows, :] = jnp.where(bits >= 0, bits, bits ^ jnp.int32(0x7FFFFFFF))

    def score_body(kj, c):
        score_block(kj, False)
        return c

    lax.fori_loop(0, qi, score_body, 0)
    score_block(qi, True)

    def count_ge(cand):
        def body(kj, cnt):
            blk = keys_ref[pl.ds(pl.multiple_of(kj * TB, TB), TB), :]
            return cnt + jnp.sum(jnp.where(blk >= cand, 1, 0).astype(jnp.int32), axis=0, keepdims=True)
        return lax.fori_loop(0, qi + 1, body, jnp.zeros((1, TB), jnp.int32))

    lo = jnp.where(count_ge(jnp.zeros((1, TB), jnp.int32)) >= topk, jnp.int32(0), jnp.int32(INT_MIN))

    def bit_body(t, lo):
        cand = lo + lax.shift_left(jnp.int32(1), jnp.int32(30) - t)
        return jnp.where(count_ge(cand) >= topk, cand, lo)

    lo = lax.fori_loop(0, 31, bit_body, lo)
    thr = jnp.maximum(lo, jnp.int32(KEY_NEG_INF + 1))

    hd = lambda h: slice(h * DSA_HEAD_DIM, (h + 1) * DSA_HEAD_DIM)
    lanes = lambda h: slice(h * TB, (h + 1) * TB)
    _softmax_reset(m_ref, l_ref, acc_ref)
    for h in range(N_DSA_HEADS):
        bias_ref[:, lanes(h)] = slopes_ref[h] * rel
        qall_ref[:, lanes(h)] = sqT_ref[0, hd(h), :]
    slope_row = jnp.concatenate([jnp.full((1, TB), slopes_ref[h], F32) for h in range(N_DSA_HEADS)], axis=1)

    def attend(kj, c):
        rows = pl.ds(pl.multiple_of(kj * TB, TB), TB)
        dist0 = ((qi - kj) * TB).astype(F32)
        s = jnp.dot(sk_ref[rows, :], qall_ref[...], preferred_element_type=F32) - bias_ref[...]
        sel = keys_ref[rows, :] >= thr
        s = jnp.where(jnp.concatenate([sel] * N_DSA_HEADS, axis=1), s, NEG_BIG)
        _softmax_update(0, s, slope_row * dist0, svT_ref[kj], m_ref, l_ref, acc_ref)
        return c

    lax.fori_loop(0, qi + 1, attend, 0)
    out = acc_ref[0] / l_ref[0]
    for h in range(N_DSA_HEADS):
        o_ref[0, hd(h), :] = out[:, lanes(h)].astype(BF16)


def _dsa_attn(fm, tok, iwT, slopes, B, S):
    nq = S // TB
    T = B * S
    topk = min(DSA_TOPK_MAX, S // 4)
    sq_blk = (2 * DIFF_W) // DSA_QW
    iq_blk = (2 * DIFF_W + DSA_QW) // IDX_QW
    sv_blk = (2 * DIFF_W + DSA_QW + IDX_QW) // DSA_HEAD_DIM
    sk_blk = DIFF_W // DSA_HEAD_DIM
    return pl.pallas_call(
        functools.partial(_dsa_attn_kernel, topk),
        grid=(B, nq),
        in_specs=[
            pl.BlockSpec(memory_space=pltpu.SMEM),
            pl.BlockSpec((1, DSA_QW, TB), lambda b, q: (b * nq + q, sq_blk, 0)),
            pl.BlockSpec((1, IDX_QW, TB), lambda b, q: (b * nq + q, iq_blk, 0)),
            pl.BlockSpec((nq, DSA_HEAD_DIM, TB), lambda b, q: (b, sv_blk, 0)),
            pl.BlockSpec((S, DSA_HEAD_DIM), lambda b, q: (b, sk_blk)),
            pl.BlockSpec((S, 2 * IDX_HEAD_DIM), lambda b, q: (b, sk_blk + 1)),
            pl.BlockSpec((1, N_IDX_HEADS, TB), lambda b, q: (b * nq + q, 0, 0)),
        ],
        out_specs=pl.BlockSpec((1, DSA_QW, TB), lambda b, q: (b * nq + q, 0, 0)),
        out_shape=jax.ShapeDtypeStruct((T // TB, DSA_QW, TB), BF16),
        scratch_shapes=[
            pltpu.VMEM((S, TB), jnp.int32),
            pltpu.VMEM((2 * IDX_HEAD_DIM, N_IDX_HEADS * TB), BF16),
            pltpu.VMEM((TB, N_DSA_HEADS * TB), F32),
            pltpu.VMEM((DSA_HEAD_DIM, N_DSA_HEADS * TB), BF16),
            pltpu.VMEM((1, 1, N_DSA_HEADS * TB), F32),
            pltpu.VMEM((1, 1, N_DSA_HEADS * TB), F32),
            pltpu.VMEM((1, DSA_HEAD_DIM, N_DSA_HEADS * TB), F32),
        ],
        compiler_params=_params(("parallel", "arbitrary")),
        name="dsa_attn",
    )(slopes, fm, fm, fm, tok, tok, iwT)


def _layer_norm_cols(y, g, b):
    mu = jnp.mean(y, axis=0, keepdims=True)
    d = y - mu
    var = jnp.mean(d * d, axis=0, keepdims=True)
    return d * lax.rsqrt(var + LN_EPS) * g + b


def _out_proj_kernel(alpha, diffT_ref, dsaT_ref, woT_ref, x_ref, g_ref, b_ref, x1T_ref, x1Tb_ref):
    nslab = diffT_ref.shape[0]
    attnT = jnp.concatenate(
        [jnp.concatenate([diffT_ref[s], dsaT_ref[s]], axis=0) for s in range(nslab)], axis=1)
    mixedT = jnp.dot(woT_ref[...], attnT, preferred_element_type=F32)
    for s in range(nslab):
        xT = x_ref[s * TB:(s + 1) * TB, :].T
        y = _layer_norm_cols(alpha * xT + mixedT[:, s * TB:(s + 1) * TB], g_ref[...], b_ref[...])
        x1T_ref[s] = y
        x1Tb_ref[s] = y.astype(BF16)


def _out_proj(diffT, dsaT, woT, x2d, g, b, alpha, tm):
    T, D = x2d.shape
    nslab = tm // TB
    fm = lambda i: (i, 0, 0)
    return pl.pallas_call(
        functools.partial(_out_proj_kernel, alpha),
        grid=(T // tm,),
        in_specs=[
            pl.BlockSpec((nslab, DIFF_W, TB), fm),
            pl.BlockSpec((nslab, DSA_QW, TB), fm),
            pl.BlockSpec((D, DIFF_W + DSA_QW), lambda i: (0, 0)),
            pl.BlockSpec((tm, D), lambda i: (i, 0)),
            pl.BlockSpec((D, 1), lambda i: (0, 0)),
            pl.BlockSpec((D, 1), lambda i: (0, 0)),
        ],
        out_specs=[pl.BlockSpec((nslab, D, TB), fm), pl.BlockSpec((nslab, D, TB), fm)],
        out_shape=[jax.ShapeDtypeStruct((T // TB, D, TB), F32), jax.ShapeDtypeStruct((T // TB, D, TB), BF16)],
        compiler_params=_params(("parallel",)),
        name="out_proj_ln1",
    )(diffT, dsaT, woT, x2d, g, b)


def _top_values(s, k):
    vals = []
    for _ in range(k):
        m = jnp.max(s, axis=0, keepdims=True)
        vals.append(m)
        s = jnp.where(s == m, -jnp.inf, s)
    return vals


def _top_values_ranked(s, k):
    vals = []
    rank = jnp.full(s.shape, float(k), F32)
    for b in range(k):
        m = jnp.max(s, axis=0, keepdims=True)
        vals.append(m)
        hit = s == m
        s = jnp.where(hit, -jnp.inf, s)
        rank = jnp.where(hit, float(b), rank)
    return vals, rank


def _bf16_pair_words(x):
    bits = pltpu.bitcast(x, jnp.uint32)
    bits = bits + jnp.uint32(0x7FFF) + (lax.shift_right_logical(bits, jnp.uint32(16)) & jnp.uint32(1))
    hi = bits & jnp.uint32(0xFFFF0000)
    return hi | lax.shift_right_logical(hi, jnp.uint32(16))


def _peer_gate_kernel(x1Tb_ref, wqT_ref, k1_ref, k2_ref, cnt_ref, rnk_ref, e1_ref, e2_ref):
    nslab = x1Tb_ref.shape[0]
    x1cat = jnp.concatenate([x1Tb_ref[s] for s in range(nslab)], axis=1)
    qT = jnp.dot(wqT_ref[...], x1cat, preferred_element_type=F32).astype(BF16)
    for h in range(PEER_HEADS):
        base = h * PEER_QDIM
        s1w = jnp.dot(k1_ref[...], qT[base:base + PEER_HALF], preferred_element_type=F32)
        s2w = jnp.dot(k2_ref[...], qT[base + PEER_HALF:base + PEER_QDIM], preferred_element_type=F32)
        for s in range(nslab):
            s1 = s1w[:, s * TB:(s + 1) * TB]
            s2 = s2w[:, s * TB:(s + 1) * TB]
            v1 = _top_values(s1, PEER_TOPK)
            v2, rank2 = _top_values_ranked(s2, PEER_TOPK)
            v1c = jnp.concatenate(v1, axis=0)
            v2c = jnp.concatenate(v2, axis=0)
            half = PEER_TOPK // 2
            cand = jnp.concatenate(
                [v1[0] + v2c] + [v1[a] + v2c[:half] for a in range(1, half)] + [v1c[half:] + v2[0]], axis=0)
            tau = _top_values(cand, PEER_TOPK)[-1]
            m1, m2 = v1[0], v2[0]
            z = jnp.sum(jnp.where(cand >= tau, jnp.exp(cand - (m1 + m2)), 0.0), axis=0, keepdims=True)
            cnt = jnp.zeros(s1.shape, F32)
            for vb in v2:
                cnt = cnt + jnp.where(s1 + vb >= tau, 1.0, 0.0)
            rows = slice(h * PEER_NKEYS, (h + 1) * PEER_NKEYS)
            cnt_ref[s, rows, :] = _bf16_pair_words(cnt)
            rnk_ref[s, rows, :] = rank2.astype(BF16)
            e1_ref[s, rows, :] = _bf16_pair_words(jnp.exp(s1 - m1) / z)
            e2_ref[s, rows, :] = jnp.exp(s2 - m2).astype(BF16)


def _peer_gate(x1Tb, wqT, k1, k2, tm):
    nblk, D, _ = x1Tb.shape
    W = PEER_HEADS * PEER_NKEYS
    nslab = tm // TB
    fm = lambda i: (i, 0, 0)
    return pl.pallas_call(
        _peer_gate_kernel,
        grid=(nblk // nslab,),
        in_specs=[
            pl.BlockSpec((nslab, D, TB), fm),
            pl.BlockSpec((PEER_HEADS * PEER_QDIM, D), lambda i: (0, 0)),
            pl.BlockSpec((PEER_NKEYS, PEER_HALF), lambda i: (0, 0)),
            pl.BlockSpec((PEER_NKEYS, PEER_HALF), lambda i: (0, 0)),
        ],
        out_specs=[pl.BlockSpec((nslab, W, TB), fm)] * 4,
        out_shape=[jax.ShapeDtypeStruct((nblk, W, TB), dt) for dt in (jnp.uint32, BF16, jnp.uint32, BF16)],
        compiler_params=_params(("parallel",)),
        name="peer_gate",
    )(x1Tb, wqT, k1, k2)


def _peer_gate_units(i0, cnt_ref, rnk_ref, e1_ref, e2_ref, g_ref):
    def pair_row(ref, s, r):
        return pltpu.bitcast(jnp.broadcast_to(ref[s, r, :], (PEER_NKEYS // 2, TB)), BF16)

    def unit(ii, s):
        def run():
            i = jnp.minimum(i0 + ii, PEER_NKEYS - 1)
            gate = jnp.zeros((PEER_NKEYS, TB), BF16)
            for hd in range(PEER_HEADS):
                r = pl.ds(hd * PEER_NKEYS + i, 1)
                keys = slice(hd * PEER_NKEYS, (hd + 1) * PEER_NKEYS)
                e2 = e2_ref[s, keys, :]
                sel = jnp.where(rnk_ref[s, keys, :] < pair_row(cnt_ref, s, r), e2, jnp.zeros_like(e2))
                gate = gate + pair_row(e1_ref, s, r) * sel
            g_ref[ii * PEER_NKEYS:(ii + 1) * PEER_NKEYS, s * TB:(s + 1) * TB] = gate
        return run
    return [unit(ii, s) for ii in range(g_ref.shape[0] // PEER_NKEYS) for s in range(cnt_ref.shape[0])]


def _peer_dense_kernel(x1Tb_ref, u_ref, vT_ref, cnt_ref, rnk_ref, e1_ref, e2_ref, yT_ref,
                       ga_ref, gb_ref, h_ref, a_ref):
    ei = pl.program_id(1)
    te, tm = ga_ref.shape
    n_i = te // PEER_NKEYS
    nslab = cnt_ref.shape[0]
    D = u_ref.shape[1]
    kc = D // 4
    rc = D // 4
    gate_args = (cnt_ref, rnk_ref, e1_ref, e2_ref)

    @pl.when(ei == 0)
    def _():
        yT_ref[...] = jnp.zeros_like(yT_ref)
        for unit in _peer_gate_units(0, *gate_args, ga_ref):
            unit()

    def h_unit(t, k):
        def run():
            xk = jnp.concatenate([x1Tb_ref[s, k * kc:(k + 1) * kc, :] for s in range(nslab)], axis=1)
            part = jnp.dot(u_ref[t * te:(t + 1) * te, k * kc:(k + 1) * kc], xk, preferred_element_type=F32)
            if k == 0:
                h_ref[t] = part
            else:
                h_ref[t] += part
        return run

    def act_unit(t, g_ref):
        def run():
            hT = h_ref[t]
            act = 0.5 * hT * (1.0 + lax.erf(hT * np.float32(1.0 / math.sqrt(2.0))))
            a_ref[t] = act.astype(BF16) * g_ref[...]
        return run

    def y_unit(t, r):
        def run():
            y = jnp.dot(vT_ref[r * rc:(r + 1) * rc, t * te:(t + 1) * te], a_ref[t], preferred_element_type=F32)
            for s in range(nslab):
                yT_ref[s, r * rc:(r + 1) * rc, :] += y[:, s * TB:(s + 1) * TB]
        return run

    gb = _peer_gate_units((2 * ei + 1) * n_i, *gate_args, gb_ref)
    ga = _peer_gate_units((2 * ei + 2) * n_i, *gate_args, ga_ref)
    hA, hB = [h_unit(0, k) for k in range(4)], [h_unit(1, k) for k in range(4)]
    yA, yB = [y_unit(0, r) for r in range(4)], [y_unit(1, r) for r in range(4)]
    actA, actB = act_unit(0, ga_ref), act_unit(1, gb_ref)
    assert len(gb) == 8
    order = [hA[0], gb[0], hA[1], gb[1], hA[2], gb[2], hA[3], gb[3],
             hB[0], actA, hB[1], gb[4], hB[2], gb[5], hB[3], gb[6],
             yA[0], gb[7], yA[1], actB, yA[2], ga[0], yA[3], ga[1],
             yB[0], ga[2], ga[3], yB[1], ga[4], ga[5], yB[2], ga[6], yB[3], ga[7]]
    for unit in order:
        unit()


def _peer_dense(x1Tb, u, vT, gates, tm, te):
    nblk, D, _ = x1Tb.shape
    E = u.shape[0]
    W = PEER_HEADS * PEER_NKEYS
    nslab = tm // TB
    fm = lambda i, e: (i, 0, 0)
    return pl.pallas_call(
        _peer_dense_kernel,
        grid=(nblk // nslab, E // (2 * te)),
        in_specs=[
            pl.BlockSpec((nslab, D, TB), fm),
            pl.BlockSpec((2 * te, D), lambda i, e: (e, 0)),
            pl.BlockSpec((D, 2 * te), lambda i, e: (0, e)),
            pl.BlockSpec((nslab, W, TB), fm),
            pl.BlockSpec((nslab, W, TB), fm),
            pl.BlockSpec((nslab, W, TB), fm),
            pl.BlockSpec((nslab, W, TB), fm),
        ],
        out_specs=pl.BlockSpec((nslab, D, TB), fm),
        out_shape=jax.ShapeDtypeStruct((nblk, D, TB), F32),
        scratch_shapes=[pltpu.VMEM((te, tm), BF16), pltpu.VMEM((te, tm), BF16),
                        pltpu.VMEM((2, te, tm), F32), pltpu.VMEM((2, te, tm), BF16)],
        compiler_params=_params(("parallel", "arbitrary")),
        name="peer_dense",
    )(x1Tb, u, vT, *gates)


def _ln2_kernel(alpha, x1T_ref, yT_ref, g_ref, b_ref, o_ref):
    for s in range(x1T_ref.shape[0]):
        z = _layer_norm_cols(alpha * x1T_ref[s] + yT_ref[s], g_ref[...], b_ref[...])
        o_ref[s * TB:(s + 1) * TB, :] = z.T


def _ln2(x1T, yT, g, b, alpha, tm):
    nblk, D, _ = x1T.shape
    nslab = tm // TB
    fm = lambda i: (i, 0, 0)
    return pl.pallas_call(
        functools.partial(_ln2_kernel, alpha),
        grid=(nblk // nslab,),
        in_specs=[
            pl.BlockSpec((nslab, D, TB), fm),
            pl.BlockSpec((nslab, D, TB), fm),
            pl.BlockSpec((D, 1), lambda i: (0, 0)),
            pl.BlockSpec((D, 1), lambda i: (0, 0)),
        ],
        out_specs=pl.BlockSpec((tm, D), lambda i: (i, 0)),
        out_shape=jax.ShapeDtypeStruct((nblk * TB, D), F32),
        compiler_params=_params(("parallel",)),
        name="ln2_out",
    )(x1T, yT, g, b)


def _tiles(T):
    tm = 2 * TB if T % (2 * TB) == 0 else TB
    return dict(proj_tm=tm, proj_tn=FM_ROWS // 3, outproj_tm=tm, dense_tm=tm, dense_te=4 * PEER_NKEYS)


def kernel(x, w_in, w_o, lambda_q1, lambda_k1, lambda_q2, lambda_k2, subln_g, ln1_g, ln1_b,
           peer_wq, peer_k1, peer_k2, peer_u, peer_v, ln2_g, ln2_b):
    B, S, D = x.shape
    T = B * S
    depth = w_in.shape[0]
    assert S % TB == 0 and w_in.shape[2] == sum(IN_SPLITS)
    alpha = float((2 * depth) ** 0.25)
    slopes_diff, slopes_dsa = (s * F32(LOG2E) for s in _alibi_slopes())
    tiles = _tiles(T)
    offs = np.cumsum((0,) + IN_SPLITS)
    col = lambda w, k: w[:, offs[k]:offs[k + 1]]

    xt = x.reshape(T, D)
    for l in range(depth):
        w = w_in[l]
        dq, dk, dv, sq, sk, sv, iq, ik, iw = (col(w, k) for k in range(9))
        wt_fm = jnp.concatenate(
            [dq * (LOG2E * DIFF_QK_HALF ** -0.5), dv, sq * (LOG2E * DSA_HEAD_DIM ** -0.5), iq, sv],
            axis=1).T.astype(BF16)
        w_tok = jnp.concatenate([dk, sk, ik, ik], axis=1).astype(BF16)
        wt_iw = (iw * ((IDX_HEAD_DIM ** -0.5) * (N_IDX_HEADS ** -0.5))).T.astype(BF16)
        fm, tok, iwT = _in_proj(xt, wt_fm, w_tok, wt_iw, tiles["proj_tm"], tiles["proj_tn"])

        lam_init = 0.8 - 0.6 * math.exp(-0.3 * l)
        lam_params = jnp.stack([lambda_q1[l], lambda_k1[l], lambda_q2[l], lambda_k2[l]]).astype(F32)
        g_col = (subln_g[l].astype(F32) * (1.0 - lam_init)).reshape(DIFF_HEAD_DIM, 1)
        diffT = _diff_attn(fm, tok, slopes_diff, lam_params, g_col, lam_init, B, S)
        dsaT = _dsa_attn(fm, tok, iwT, slopes_dsa, B, S)

        colv = lambda p: p.reshape(D, 1).astype(F32)
        x1T, x1Tb = _out_proj(diffT, dsaT, w_o[l].T.astype(BF16), xt, colv(ln1_g[l]), colv(ln1_b[l]), alpha,
                              tiles["outproj_tm"])

        gates = _peer_gate(x1Tb, peer_wq[l].T.astype(BF16), peer_k1[l].astype(BF16), peer_k2[l].astype(BF16),
                           tiles["outproj_tm"])
        yT = _peer_dense(x1Tb, peer_u[l].astype(BF16), peer_v[l].T.astype(BF16), gates,
                         tiles["dense_tm"], tiles["dense_te"])
        xt = _ln2(x1T, yT, colv(ln2_g[l]), colv(ln2_b[l]), alpha, tiles["outproj_tm"])
    return xt.reshape(B, S, D)
```

```python
import functools
import math

import jax
import jax.numpy as jnp
import numpy as np
from jax import lax
from jax.experimental import pallas as pl
from jax.experimental.pallas import tpu as pltpu

F32 = jnp.float32
BF16 = jnp.bfloat16

N_DIFF_HEADS = 8
DIFF_HEAD_DIM = 128
DIFF_QK_HALF = DIFF_HEAD_DIM // 2
N_DSA_HEADS = 8
DSA_HEAD_DIM = 128
N_IDX_HEADS = 16
IDX_HEAD_DIM = 64
DSA_TOPK_MAX = 256
PEER_HEADS = 8
PEER_NKEYS = 128
PEER_QDIM = 256
PEER_HALF = PEER_QDIM // 2
PEER_TOPK = 16
LN_EPS = 1e-5
RMS_EPS = 1e-5

DIFF_W = N_DIFF_HEADS * DIFF_HEAD_DIM
DSA_QW = N_DSA_HEADS * DSA_HEAD_DIM
IDX_QW = N_IDX_HEADS * IDX_HEAD_DIM
IN_SPLITS = (DIFF_W, DIFF_W, DIFF_W, DSA_QW, DSA_HEAD_DIM, DSA_HEAD_DIM, IDX_QW, IDX_HEAD_DIM, N_IDX_HEADS)

TB = 256
LANE = 128
FM_ROWS = DIFF_W * 2 + DSA_QW + IDX_QW + DSA_HEAD_DIM
TOK_COLS = DIFF_W + DSA_HEAD_DIM + 2 * IDX_HEAD_DIM
VMEM_LIMIT = 56 * 1024 * 1024

LOG2E = math.log2(math.e)
NEG_BIG = -1e30
KEY_NEG_INF = (0xFF800000 ^ 0x7FFFFFFF) - 2 ** 32
INT_MIN = -(2 ** 31)

NT_DIMS = (((1,), (1,)), ((), ()))
TN_DIMS = (((0,), (0,)), ((), ()))


def _alibi_slopes():
    n = N_DIFF_HEADS + N_DSA_HEADS
    s = 2.0 ** (-8.0 * np.arange(1, n + 1) / n)
    return (jnp.asarray(s[0::2], dtype=F32), jnp.asarray(s[1::2], dtype=F32))


def _params(sem):
    return pltpu.CompilerParams(dimension_semantics=sem, vmem_limit_bytes=VMEM_LIMIT)


def _in_proj_kernel(x_ref, wt_ref, wtok_ref, wiw_ref, fm_ref, tok_ref, iw_ref, xb_ref):
    nslab = fm_ref.shape[0]

    @pl.when(pl.program_id(1) == 0)
    def _():
        xb = x_ref[...].astype(BF16)
        xb_ref[...] = xb
        tok_ref[...] = jnp.dot(xb, wtok_ref[...], preferred_element_type=F32).astype(BF16)
        iw = lax.dot_general(wiw_ref[...], xb, NT_DIMS, preferred_element_type=F32)
        for s in range(nslab):
            iw_ref[s] = iw[:, s * TB:(s + 1) * TB]

    r = lax.dot_general(wt_ref[...], xb_ref[...], NT_DIMS, preferred_element_type=F32)
    for s in range(nslab):
        fm_ref[s] = r[:, s * TB:(s + 1) * TB].astype(BF16)


def _in_proj(x2d, wt_fm, w_tok, wt_iw, tm, tn):
    T, D = x2d.shape
    nslab = tm // TB
    return pl.pallas_call(
        _in_proj_kernel,
        grid=(T // tm, FM_ROWS // tn),
        in_specs=[
            pl.BlockSpec((tm, D), lambda i, j: (i, 0)),
            pl.BlockSpec((tn, D), lambda i, j: (j, 0)),
            pl.BlockSpec((D, TOK_COLS), lambda i, j: (0, 0)),
            pl.BlockSpec((N_IDX_HEADS, D), lambda i, j: (0, 0)),
        ],
        out_specs=[
            pl.BlockSpec((nslab, tn, TB), lambda i, j: (i, j, 0)),
            pl.BlockSpec((tm, TOK_COLS), lambda i, j: (i, 0)),
            pl.BlockSpec((nslab, N_IDX_HEADS, TB), lambda i, j: (i, 0, 0)),
        ],
        out_shape=[
            jax.ShapeDtypeStruct((T // TB, FM_ROWS, TB), BF16),
            jax.ShapeDtypeStruct((T, TOK_COLS), BF16),
            jax.ShapeDtypeStruct((T // TB, N_IDX_HEADS, TB), F32),
        ],
        scratch_shapes=[pltpu.VMEM((tm, D), BF16)],
        compiler_params=_params(("parallel", "arbitrary")),
        name="in_proj",
    )(x2d, wt_fm, w_tok, wt_iw)


def _rel_pos():
    kk = lax.broadcasted_iota(jnp.int32, (TB, TB), 0)
    qq = lax.broadcasted_iota(jnp.int32, (TB, TB), 1)
    return (qq - kk).astype(F32)


def _softmax_reset(m_ref, l_ref, acc_ref):
    m_ref[...] = jnp.full(m_ref.shape, NEG_BIG, F32)
    l_ref[...] = jnp.zeros(l_ref.shape, F32)
    acc_ref[...] = jnp.zeros(acc_ref.shape, F32)


def _softmax_update(idx, s, shift, vt, m_ref, l_ref, acc_ref):
    m_old = m_ref[idx]
    m_new = jnp.maximum(m_old, jnp.max(s, axis=0, keepdims=True) - shift)
    alpha = jnp.exp2(m_old - m_new)
    p = jnp.exp2(s - (m_new + shift))
    l_ref[idx] = alpha * l_ref[idx] + jnp.sum(p, axis=0, keepdims=True)
    acc_ref[idx] = alpha * acc_ref[idx] + jnp.dot(vt, p.astype(BF16), preferred_element_type=F32)
    m_ref[idx] = m_new


def _diff_attn_kernel(lam_init, slopes_ref, lamp_ref, qT_ref, k_ref, vT_ref, g_ref, o_ref,
                      qm_ref, bias_ref, m_ref, l_ref, acc_ref, s_ref, p_ref, alpha_ref):
    qi = pl.program_id(1)
    lp = lamp_ref[...]
    lam = (jnp.exp(jnp.sum(lp[0:1] * lp[1:2], axis=1, keepdims=True))
           - jnp.exp(jnp.sum(lp[2:3] * lp[3:4], axis=1, keepdims=True)) + lam_init)
    rel = _rel_pos()
    row = lax.broadcasted_iota(jnp.int32, (DIFF_HEAD_DIM, TB), 0)
    hd = lambda h: slice(h * DIFF_HEAD_DIM, (h + 1) * DIFF_HEAD_DIM)

    _softmax_reset(m_ref, l_ref, acc_ref)
    rel2 = jnp.concatenate([rel, rel], axis=1)
    for h in range(N_DIFF_HEADS):
        qT = qT_ref[0, hd(h), :]
        zero = jnp.zeros_like(qT)
        qm_ref[h] = jnp.concatenate(
            [jnp.where(row < DIFF_QK_HALF, qT, zero), jnp.where(row >= DIFF_QK_HALF, qT, zero)], axis=1)
        bias_ref[h] = slopes_ref[h] * rel2

    def step(kj, diag):
        rows = pl.ds(pl.multiple_of(kj * TB, TB), TB)
        dist0 = ((qi - kj) * TB).astype(F32)
        for h in range(N_DIFF_HEADS):
            s_ref[h] = jnp.dot(k_ref[rows, hd(h)], qm_ref[h], preferred_element_type=F32)
        for h in range(N_DIFF_HEADS):
            s = s_ref[h] - bias_ref[h]
            if diag:
                s = jnp.where(rel2 >= 0, s, NEG_BIG)
            shift = slopes_ref[h] * dist0
            m_old = m_ref[h]
            m_new = jnp.maximum(m_old, jnp.max(s, axis=0, keepdims=True) - shift)
            alpha = jnp.exp2(m_old - m_new)
            p = jnp.exp2(s - (m_new + shift))
            l_ref[h] = alpha * l_ref[h] + jnp.sum(p, axis=0, keepdims=True)
            p_ref[h] = p.astype(BF16)
            alpha_ref[h] = alpha
            m_ref[h] = m_new
        for h in range(N_DIFF_HEADS):
            acc_ref[h] = alpha_ref[h] * acc_ref[h] + jnp.dot(vT_ref[kj, hd(h), :], p_ref[h],
                                                               preferred_element_type=F32)

    def body(kj, c):
        step(kj, False)
        return c

    lax.fori_loop(0, qi, body, 0)
    step(qi, True)
    for h in range(N_DIFF_HEADS):
        w = acc_ref[h] / l_ref[h]
        out = w[:, :TB] - lam * w[:, TB:]
        ms = jnp.mean(out * out, axis=0, keepdims=True)
        o_ref[0, hd(h), :] = (out * lax.rsqrt(ms + RMS_EPS) * g_ref[...]).astype(BF16)


def _diff_attn(fm, tok, slopes, lam_params, g_col, lam_init, B, S):
    nq = S // TB
    T = B * S
    nchain = 2 * N_DIFF_HEADS
    return pl.pallas_call(
        functools.partial(_diff_attn_kernel, lam_init),
        grid=(B, nq),
        in_specs=[
            pl.BlockSpec(memory_space=pltpu.SMEM),
            pl.BlockSpec((4, DIFF_QK_HALF), lambda b, q: (0, 0)),
            pl.BlockSpec((1, DIFF_W, TB), lambda b, q: (b * nq + q, 0, 0)),
            pl.BlockSpec((S, DIFF_W), lambda b, q: (b, 0)),
            pl.BlockSpec((nq, DIFF_W, TB), lambda b, q: (b, 1, 0)),
            pl.BlockSpec((DIFF_HEAD_DIM, 1), lambda b, q: (0, 0)),
        ],
        out_specs=pl.BlockSpec((1, DIFF_W, TB), lambda b, q: (b * nq + q, 0, 0)),
        out_shape=jax.ShapeDtypeStruct((T // TB, DIFF_W, TB), BF16),
        scratch_shapes=[
            pltpu.VMEM((N_DIFF_HEADS, DIFF_HEAD_DIM, 2 * TB), BF16),
            pltpu.VMEM((N_DIFF_HEADS, TB, 2 * TB), F32),
            pltpu.VMEM((N_DIFF_HEADS, 1, 2 * TB), F32),
            pltpu.VMEM((N_DIFF_HEADS, 1, 2 * TB), F32),
            pltpu.VMEM((N_DIFF_HEADS, DIFF_HEAD_DIM, 2 * TB), F32),
            pltpu.VMEM((N_DIFF_HEADS, TB, 2 * TB), F32),
            pltpu.VMEM((N_DIFF_HEADS, TB, 2 * TB), BF16),
            pltpu.VMEM((N_DIFF_HEADS, 1, 2 * TB), F32),
        ],
        compiler_params=_params(("parallel", "arbitrary")),
        name="diff_attn",
    )(slopes, lam_params, fm, tok, fm, g_col)


def _dsa_attn_kernel(topk, slopes_ref, sqT_ref, iqT_ref, svT_ref, sk_ref, ikk_ref, iwT_ref, o_ref, keys_ref,
                     iqall_ref, bias_ref, qall_ref, m_ref, l_ref, acc_ref):
    qi = pl.program_id(1)
    rel = _rel_pos()
    iw = iwT_ref[0]
    row = lax.broadcasted_iota(jnp.int32, (2 * IDX_HEAD_DIM, TB), 0)

    for p in range(N_IDX_HEADS // 2):
        pair = iqT_ref[0, p * 2 * IDX_HEAD_DIM:(p + 1) * 2 * IDX_HEAD_DIM, :]
        zero = jnp.zeros_like(pair)
        iqall_ref[:, (2 * p) * TB:(2 * p + 1) * TB] = jnp.where(row < IDX_HEAD_DIM, pair, zero)
        iqall_ref[:, (2 * p + 1) * TB:(2 * p + 2) * TB] = jnp.where(row >= IDX_HEAD_DIM, pair, zero)

    def score_block(kj, diag):
        rows = pl.ds(pl.multiple_of(kj * TB, TB), TB)
        ikk = ikk_ref[rows, :]
        acc = jnp.zeros((TB, TB), F32)
        half = N_IDX_HEADS // 2
        for g in range(2):
            r = jnp.dot(ikk, iqall_ref[:, g * half * TB:(g + 1) * half * TB], preferred_element_type=F32)
            for j in range(half):
                hh = g * half + j
                acc = acc + iw[hh:hh + 1, :] * jnp.maximum(r[:, j * TB:(j + 1) * TB], 0.0)
        if diag:
            acc = jnp.where(rel >= 0, acc, -jnp.inf)
        bits = pltpu.bitcast(acc, jnp.int32)
        keys_ref[rows, :] = jnp.where(bits >= 0, bits, bits ^ jnp.int32(0x7FFFFFFF))

    def score_body(kj, c):
        score_block(kj, False)
        return c

    lax.fori_loop(0, qi, score_body, 0)
    score_block(qi, True)

    def count_ge(cand):
        def body(kj, cnt):
            blk = keys_ref[pl.ds(pl.multiple_of(kj * TB, TB), TB), :]
            return cnt + jnp.sum(jnp.where(blk >= cand, 1, 0).astype(jnp.int32), axis=0, keepdims=True)
        return lax.fori_loop(0, qi + 1, body, jnp.zeros((1, TB), jnp.int32))

    lo = jnp.where(count_ge(jnp.zeros((1, TB), jnp.int32)) >= topk, jnp.int32(0), jnp.int32(INT_MIN))

    def bit_body(t, lo):
        cand = lo + lax.shift_left(jnp.int32(1), jnp.int32(30) - t)
        return jnp.where(count_ge(cand) >= topk, cand, lo)

    lo = lax.fori_loop(0, 31, bit_body, lo)
    thr = jnp.maximum(lo, jnp.int32(KEY_NEG_INF + 1))

    hd = lambda h: slice(h * DSA_HEAD_DIM, (h + 1) * DSA_HEAD_DIM)
    lanes = lambda h: slice(h * TB, (h + 1) * TB)
    _softmax_reset(m_ref, l_ref, acc_ref)
    for h in range(N_DSA_HEADS):
        bias_ref[:, lanes(h)] = slopes_ref[h] * rel
        qall_ref[:, lanes(h)] = sqT_ref[0, hd(h), :]
    slope_row = jnp.concatenate([jnp.full((1, TB), slopes_ref[h], F32) for h in range(N_DSA_HEADS)], axis=1)

    def attend(kj, c):
        rows = pl.ds(pl.multiple_of(kj * TB, TB), TB)
        dist0 = ((qi - kj) * TB).astype(F32)
        s = jnp.dot(sk_ref[rows, :], qall_ref[...], preferred_element_type=F32) - bias_ref[...]
        sel = keys_ref[rows, :] >= thr
        s = jnp.where(jnp.concatenate([sel] * N_DSA_HEADS, axis=1), s, NEG_BIG)
        _softmax_update(0, s, slope_row * dist0, svT_ref[kj], m_ref, l_ref, acc_ref)
        return c

    lax.fori_loop(0, qi + 1, attend, 0)
    out = acc_ref[0] / l_ref[0]
    for h in range(N_DSA_HEADS):
        o_ref[0, hd(h), :] = out[:, lanes(h)].astype(BF16)


def _dsa_attn(fm, tok, iwT, slopes, B, S):
    nq = S // TB
    T = B * S
    topk = min(DSA_TOPK_MAX, S // 4)
    sq_blk = (2 * DIFF_W) // DSA_QW
    iq_blk = (2 * DIFF_W + DSA_QW) // IDX_QW
    sv_blk = (2 * DIFF_W + DSA_QW + IDX_QW) // DSA_HEAD_DIM
    sk_blk = DIFF_W // DSA_HEAD_DIM
    return pl.pallas_call(
        functools.partial(_dsa_attn_kernel, topk),
        grid=(B, nq),
        in_specs=[
            pl.BlockSpec(memory_space=pltpu.SMEM),
            pl.BlockSpec((1, DSA_QW, TB), lambda b, q: (b * nq + q, sq_blk, 0)),
            pl.BlockSpec((1, IDX_QW, TB), lambda b, q: (b * nq + q, iq_blk, 0)),
            pl.BlockSpec((nq, DSA_HEAD_DIM, TB), lambda b, q: (b, sv_blk, 0)),
            pl.BlockSpec((S, DSA_HEAD_DIM), lambda b, q: (b, sk_blk)),
            pl.BlockSpec((S, 2 * IDX_HEAD_DIM), lambda b, q: (b, sk_blk + 1)),
            pl.BlockSpec((1, N_IDX_HEADS, TB), lambda b, q: (b * nq + q, 0, 0)),
        ],
        out_specs=pl.BlockSpec((1, DSA_QW, TB), lambda b, q: (b * nq + q, 0, 0)),
        out_shape=jax.ShapeDtypeStruct((T // TB, DSA_QW, TB), BF16),
        scratch_shapes=[
            pltpu.VMEM((S, TB), jnp.int32),
            pltpu.VMEM((2 * IDX_HEAD_DIM, N_IDX_HEADS * TB), BF16),
            pltpu.VMEM((TB, N_DSA_HEADS * TB), F32),
            pltpu.VMEM((DSA_HEAD_DIM, N_DSA_HEADS * TB), BF16),
            pltpu.VMEM((1, 1, N_DSA_HEADS * TB), F32),
            pltpu.VMEM((1, 1, N_DSA_HEADS * TB), F32),
            pltpu.VMEM((1, DSA_HEAD_DIM, N_DSA_HEADS * TB), F32),
        ],
        compiler_params=_params(("parallel", "arbitrary")),
        name="dsa_attn",
    )(slopes, fm, fm, fm, tok, tok, iwT)


def _layer_norm_cols(y, g, b):
    mu = jnp.mean(y, axis=0, keepdims=True)
    d = y - mu
    var = jnp.mean(d * d, axis=0, keepdims=True)
    return d * lax.rsqrt(var + LN_EPS) * g + b


def _out_proj_kernel(alpha, diffT_ref, dsaT_ref, woT_ref, x_ref, g_ref, b_ref, x1T_ref, x1Tb_ref):
    nslab = diffT_ref.shape[0]
    attnT = jnp.concatenate(
        [jnp.concatenate([diffT_ref[s], dsaT_ref[s]], axis=0) for s in range(nslab)], axis=1)
    mixedT = jnp.dot(woT_ref[...], attnT, preferred_element_type=F32)
    for s in range(nslab):
        xT = x_ref[s * TB:(s + 1) * TB, :].T
        y = _layer_norm_cols(alpha * xT + mixedT[:, s * TB:(s + 1) * TB], g_ref[...], b_ref[...])
        x1T_ref[s] = y
        x1Tb_ref[s] = y.astype(BF16)


def _out_proj(diffT, dsaT, woT, x2d, g, b, alpha, tm):
    T, D = x2d.shape
    nslab = tm // TB
    fm = lambda i: (i, 0, 0)
    return pl.pallas_call(
        functools.partial(_out_proj_kernel, alpha),
        grid=(T // tm,),
        in_specs=[
            pl.BlockSpec((nslab, DIFF_W, TB), fm),
            pl.BlockSpec((nslab, DSA_QW, TB), fm),
            pl.BlockSpec((D, DIFF_W + DSA_QW), lambda i: (0, 0)),
            pl.BlockSpec((tm, D), lambda i: (i, 0)),
            pl.BlockSpec((D, 1), lambda i: (0, 0)),
            pl.BlockSpec((D, 1), lambda i: (0, 0)),
        ],
        out_specs=[pl.BlockSpec((nslab, D, TB), fm), pl.BlockSpec((nslab, D, TB), fm)],
        out_shape=[jax.ShapeDtypeStruct((T // TB, D, TB), F32), jax.ShapeDtypeStruct((T // TB, D, TB), BF16)],
        compiler_params=_params(("parallel",)),
        name="out_proj_ln1",
    )(diffT, dsaT, woT, x2d, g, b)


def _top_values(s, k):
    vals = []
    for _ in range(k):
        m = jnp.max(s, axis=0, keepdims=True)
        vals.append(m)
        s = jnp.where(s == m, -jnp.inf, s)
    return vals


def _top_values_ranked(s, k):
    vals = []
    rank = jnp.full(s.shape, float(k), F32)
    for b in range(k):
        m = jnp.max(s, axis=0, keepdims=True)
        vals.append(m)
        hit = s == m
        s = jnp.where(hit, -jnp.inf, s)
        rank = jnp.where(hit, float(b), rank)
    return vals, rank


def _bf16_pair_words(x):
    bits = pltpu.bitcast(x, jnp.uint32)
    bits = bits + jnp.uint32(0x7FFF) + (lax.shift_right_logical(bits, jnp.uint32(16)) & jnp.uint32(1))
    hi = bits & jnp.uint32(0xFFFF0000)
    return hi | lax.shift_right_logical(hi, jnp.uint32(16))


def _peer_gate_kernel(x1Tb_ref, wqT_ref, k1_ref, k2_ref, cnt_ref, rnk_ref, e1_ref, e2_ref):
    nslab = x1Tb_ref.shape[0]
    x1cat = jnp.concatenate([x1Tb_ref[s] for s in range(nslab)], axis=1)
    qT = jnp.dot(wqT_ref[...], x1cat, preferred_element_type=F32).astype(BF16)
    for h in range(PEER_HEADS):
        base = h * PEER_QDIM
        s1w = jnp.dot(k1_ref[...], qT[base:base + PEER_HALF], preferred_element_type=F32)
        s2w = jnp.dot(k2_ref[...], qT[base + PEER_HALF:base + PEER_QDIM], preferred_element_type=F32)
        for s in range(nslab):
            s1 = s1w[:, s * TB:(s + 1) * TB]
            s2 = s2w[:, s * TB:(s + 1) * TB]
            v1 = _top_values(s1, PEER_TOPK)
            v2, rank2 = _top_values_ranked(s2, PEER_TOPK)
            v1c = jnp.concatenate(v1, axis=0)
            v2c = jnp.concatenate(v2, axis=0)
            half = PEER_TOPK // 2
            cand = jnp.concatenate(
                [v1[0] + v2c] + [v1[a] + v2c[:half] for a in range(1, half)] + [v1c[half:] + v2[0]], axis=0)
            tau = _top_values(cand, PEER_TOPK)[-1]
            m1, m2 = v1[0], v2[0]
            z = jnp.sum(jnp.where(cand >= tau, jnp.exp(cand - (m1 + m2)), 0.0), axis=0, keepdims=True)
            cnt = jnp.zeros(s1.shape, F32)
            for vb in v2:
                cnt = cnt + jnp.where(s1 + vb >= tau, 1.0, 0.0)
            rows = slice(h * PEER_NKEYS, (h + 1) * PEER_NKEYS)
            cnt_ref[s, rows, :] = _bf16_pair_words(cnt)
            rnk_ref[s, rows, :] = rank2.astype(BF16)
            e1_ref[s, rows, :] = _bf16_pair_words(jnp.exp(s1 - m1) / z)
            e2_ref[s, rows, :] = jnp.exp(s2 - m2).astype(BF16)


def _peer_gate(x1Tb, wqT, k1, k2, tm):
    nblk, D, _ = x1Tb.shape
    W = PEER_HEADS * PEER_NKEYS
    nslab = tm // TB
    fm = lambda i: (i, 0, 0)
    return pl.pallas_call(
        _peer_gate_kernel,
        grid=(nblk // nslab,),
        in_specs=[
            pl.BlockSpec((nslab, D, TB), fm),
            pl.BlockSpec((PEER_HEADS * PEER_QDIM, D), lambda i: (0, 0)),
            pl.BlockSpec((PEER_NKEYS, PEER_HALF), lambda i: (0, 0)),
            pl.BlockSpec((PEER_NKEYS, PEER_HALF), lambda i: (0, 0)),
        ],
        out_specs=[pl.BlockSpec((nslab, W, TB), fm)] * 4,
        out_shape=[jax.ShapeDtypeStruct((nblk, W, TB), dt) for dt in (jnp.uint32, BF16, jnp.uint32, BF16)],
        compiler_params=_params(("parallel",)),
        name="peer_gate",
    )(x1Tb, wqT, k1, k2)


def _peer_gate_units(i0, cnt_ref, rnk_ref, e1_ref, e2_ref, g_ref):
    def pair_row(ref, s, r):
        return pltpu.bitcast(jnp.broadcast_to(ref[s, r, :], (PEER_NKEYS // 2, TB)), BF16)

    def unit(ii, s):
        def run():
            i = jnp.minimum(i0 + ii, PEER_NKEYS - 1)
            gate = jnp.zeros((PEER_NKEYS, TB), BF16)
            for hd in range(PEER_HEADS):
                r = pl.ds(hd * PEER_NKEYS + i, 1)
                keys = slice(hd * PEER_NKEYS, (hd + 1) * PEER_NKEYS)
                e2 = e2_ref[s, keys, :]
                sel = jnp.where(rnk_ref[s, keys, :] < pair_row(cnt_ref, s, r), e2, jnp.zeros_like(e2))
                gate = gate + pair_row(e1_ref, s, r) * sel
            g_ref[ii * PEER_NKEYS:(ii + 1) * PEER_NKEYS, s * TB:(s + 1) * TB] = gate
        return run
    return [unit(ii, s) for ii in range(g_ref.shape[0] // PEER_NKEYS) for s in range(cnt_ref.shape[0])]


def _peer_dense_kernel(x1Tb_ref, u_ref, vT_ref, cnt_ref, rnk_ref, e1_ref, e2_ref, yT_ref,
                       ga_ref, gb_ref, h_ref, a_ref):
    ei = pl.program_id(1)
    te, tm = ga_ref.shape
    n_i = te // PEER_NKEYS
    nslab = cnt_ref.shape[0]
    D = u_ref.shape[1]
    kc = D // 4
    rc = D // 4
    gate_args = (cnt_ref, rnk_ref, e1_ref, e2_ref)

    @pl.when(ei == 0)
    def _():
        yT_ref[...] = jnp.zeros_like(yT_ref)
        for unit in _peer_gate_units(0, *gate_args, ga_ref):
            unit()

    def h_unit(t, k):
        def run():
            xk = jnp.concatenate([x1Tb_ref[s, k * kc:(k + 1) * kc, :] for s in range(nslab)], axis=1)
            part = jnp.dot(u_ref[t * te:(t + 1) * te, k * kc:(k + 1) * kc], xk, preferred_element_type=F32)
            if k == 0:
                h_ref[t] = part
            else:
                h_ref[t] += part
        return run

    def act_unit(t, g_ref):
        def run():
            hT = h_ref[t]
            act = 0.5 * hT * (1.0 + lax.erf(hT * np.float32(1.0 / math.sqrt(2.0))))
            a_ref[t] = (act * g_ref[...].astype(F32)).astype(BF16)
        return run

    def y_unit(t, r):
        def run():
            y = jnp.dot(vT_ref[r * rc:(r + 1) * rc, t * te:(t + 1) * te], a_ref[t], preferred_element_type=F32)
            for s in range(nslab):
                yT_ref[s, r * rc:(r + 1) * rc, :] += y[:, s * TB:(s + 1) * TB]
        return run

    gb = _peer_gate_units((2 * ei + 1) * n_i, *gate_args, gb_ref)
    ga = _peer_gate_units((2 * ei + 2) * n_i, *gate_args, ga_ref)
    hA, hB = [h_unit(0, k) for k in range(4)], [h_unit(1, k) for k in range(4)]
    yA, yB = [y_unit(0, r) for r in range(4)], [y_unit(1, r) for r in range(4)]
    actA, actB = act_unit(0, ga_ref), act_unit(1, gb_ref)
    assert len(gb) == 8
    order = [hA[0], gb[0], hA[1], gb[1], hA[2], gb[2], hA[3], gb[3],
             hB[0], actA, hB[1], gb[4], hB[2], gb[5], hB[3], gb[6],
             yA[0], gb[7], yA[1], actB, yA[2], ga[0], yA[3], ga[1],
             yB[0], ga[2], ga[3], yB[1], ga[4], ga[5], yB[2], ga[6], yB[3], ga[7]]
    for unit in order:
        unit()


def _peer_dense(x1Tb, u, vT, gates, tm, te):
    nblk, D, _ = x1Tb.shape
    E = u.shape[0]
    W = PEER_HEADS * PEER_NKEYS
    nslab = tm // TB
    fm = lambda i, e: (i, 0, 0)
    return pl.pallas_call(
        _peer_dense_kernel,
        grid=(nblk // nslab, E // (2 * te)),
        in_specs=[
            pl.BlockSpec((nslab, D, TB), fm),
            pl.BlockSpec((2 * te, D), lambda i, e: (e, 0)),
            pl.BlockSpec((D, 2 * te), lambda i, e: (0, e)),
            pl.BlockSpec((nslab, W, TB), fm),
            pl.BlockSpec((nslab, W, TB), fm),
            pl.BlockSpec((nslab, W, TB), fm),
            pl.BlockSpec((nslab, W, TB), fm),
        ],
        out_specs=pl.BlockSpec((nslab, D, TB), fm),
        out_shape=jax.ShapeDtypeStruct((nblk, D, TB), F32),
        scratch_shapes=[pltpu.VMEM((te, tm), BF16), pltpu.VMEM((te, tm), BF16),
                        pltpu.VMEM((2, te, tm), F32), pltpu.VMEM((2, te, tm), BF16)],
        compiler_params=_params(("parallel", "arbitrary")),
        name="peer_dense",
    )(x1Tb, u, vT, *gates)


def _ln2_kernel(alpha, x1T_ref, yT_ref, g_ref, b_ref, o_ref):
    for s in range(x1T_ref.shape[0]):
        z = _layer_norm_cols(alpha * x1T_ref[s] + yT_ref[s], g_ref[...], b_ref[...])
        o_ref[s * TB:(s + 1) * TB, :] = z.T


def _ln2(x1T, yT, g, b, alpha, tm):
    nblk, D, _ = x1T.shape
    nslab = tm // TB
    fm = lambda i: (i, 0, 0)
    return pl.pallas_call(
        functools.partial(_ln2_kernel, alpha),
        grid=(nblk // nslab,),
        in_specs=[
            pl.BlockSpec((nslab, D, TB), fm),
            pl.BlockSpec((nslab, D, TB), fm),
            pl.BlockSpec((D, 1), lambda i: (0, 0)),
            pl.BlockSpec((D, 1), lambda i: (0, 0)),
        ],
        out_specs=pl.BlockSpec((tm, D), lambda i: (i, 0)),
        out_shape=jax.ShapeDtypeStruct((nblk * TB, D), F32),
        compiler_params=_params(("parallel",)),
        name="ln2_out",
    )(x1T, yT, g, b)


def _tiles(T):
    tm = 2 * TB if T % (2 * TB) == 0 else TB
    return dict(proj_tm=tm, proj_tn=FM_ROWS // 3, outproj_tm=tm, dense_tm=tm, dense_te=4 * PEER_NKEYS)


def kernel(x, w_in, w_o, lambda_q1, lambda_k1, lambda_q2, lambda_k2, subln_g, ln1_g, ln1_b,
           peer_wq, peer_k1, peer_k2, peer_u, peer_v, ln2_g, ln2_b):
    B, S, D = x.shape
    T = B * S
    depth = w_in.shape[0]
    assert S % TB == 0 and w_in.shape[2] == sum(IN_SPLITS)
    alpha = float((2 * depth) ** 0.25)
    slopes_diff, slopes_dsa = (s * F32(LOG2E) for s in _alibi_slopes())
    tiles = _tiles(T)
    offs = np.cumsum((0,) + IN_SPLITS)
    col = lambda w, k: w[:, offs[k]:offs[k + 1]]

    xt = x.reshape(T, D)
    for l in range(depth):
        w = w_in[l]
        dq, dk, dv, sq, sk, sv, iq, ik, iw = (col(w, k) for k in range(9))
        wt_fm = jnp.concatenate(
            [dq * (LOG2E * DIFF_QK_HALF ** -0.5), dv, sq * (LOG2E * DSA_HEAD_DIM ** -0.5), iq, sv],
            axis=1).T.astype(BF16)
        w_tok = jnp.concatenate([dk, sk, ik, ik], axis=1).astype(BF16)
        wt_iw = (iw * ((IDX_HEAD_DIM ** -0.5) * (N_IDX_HEADS ** -0.5))).T.astype(BF16)
        fm, tok, iwT = _in_proj(xt, wt_fm, w_tok, wt_iw, tiles["proj_tm"], tiles["proj_tn"])

        lam_init = 0.8 - 0.6 * math.exp(-0.3 * l)
        lam_params = jnp.stack([lambda_q1[l], lambda_k1[l], lambda_q2[l], lambda_k2[l]]).astype(F32)
        g_col = (subln_g[l].astype(F32) * (1.0 - lam_init)).reshape(DIFF_HEAD_DIM, 1)
        diffT = _diff_attn(fm, tok, slopes_diff, lam_params, g_col, lam_init, B, S)
        dsaT = _dsa_attn(fm, tok, iwT, slopes_dsa, B, S)

        colv = lambda p: p.reshape(D, 1).astype(F32)
        x1T, x1Tb = _out_proj(diffT, dsaT, w_o[l].T.astype(BF16), xt, colv(ln1_g[l]), colv(ln1_b[l]), alpha,
                              tiles["outproj_tm"])

        gates = _peer_gate(x1Tb, peer_wq[l].T.astype(BF16), peer_k1[l].astype(BF16), peer_k2[l].astype(BF16),
                           tiles["outproj_tm"])
        yT = _peer_dense(x1Tb, peer_u[l].astype(BF16), peer_v[l].T.astype(BF16), gates,
                         tiles["dense_tm"], tiles["dense_te"])
        xt = _ln2(x1T, yT, colv(ln2_g[l]), colv(ln2_b[l]), alpha, tiles["outproj_tm"])
    return xt.reshape(B, S, D)
```

```python
import functools
import math

import jax
import jax.numpy as jnp
import numpy as np
from jax import lax
from jax.experimental import pallas as pl
from jax.experimental.pallas import tpu as pltpu

F32 = jnp.float32
BF16 = jnp.bfloat16

N_DIFF_HEADS = 8
DIFF_HEAD_DIM = 128
DIFF_QK_HALF = DIFF_HEAD_DIM // 2
N_DSA_HEADS = 8
DSA_HEAD_DIM = 128
N_IDX_HEADS = 16
IDX_HEAD_DIM = 64
DSA_TOPK_MAX = 256
PEER_HEADS = 8
PEER_NKEYS = 128
PEER_QDIM = 256
PEER_HALF = PEER_QDIM // 2
PEER_TOPK = 16
LN_EPS = 1e-5
RMS_EPS = 1e-5

DIFF_W = N_DIFF_HEADS * DIFF_HEAD_DIM
DSA_QW = N_DSA_HEADS * DSA_HEAD_DIM
IDX_QW = N_IDX_HEADS * IDX_HEAD_DIM
IN_SPLITS = (DIFF_W, DIFF_W, DIFF_W, DSA_QW, DSA_HEAD_DIM, DSA_HEAD_DIM, IDX_QW, IDX_HEAD_DIM, N_IDX_HEADS)

TB = 256
FM_ROWS = DIFF_W * 2 + DSA_QW + IDX_QW + DSA_HEAD_DIM
TOK_COLS = DIFF_W + DSA_HEAD_DIM + 2 * IDX_HEAD_DIM
VMEM_LIMIT = 56 * 1024 * 1024

LOG2E = math.log2(math.e)
NEG_BIG = -1e30
KEY_NEG_INF = (0xFF800000 ^ 0x7FFFFFFF) - 2 ** 32
INT_MIN = -(2 ** 31)

NT_DIMS = (((1,), (1,)), ((), ()))


def _alibi_slopes():
    n = N_DIFF_HEADS + N_DSA_HEADS
    s = 2.0 ** (-8.0 * np.arange(1, n + 1) / n)
    return (jnp.asarray(s[0::2], dtype=F32), jnp.asarray(s[1::2], dtype=F32))


def _params(sem):
    return pltpu.CompilerParams(dimension_semantics=sem, vmem_limit_bytes=VMEM_LIMIT)


def _in_proj_kernel(x_ref, wt_ref, wtok_ref, wiw_ref, fm_ref, tok_ref, iw_ref, xb_ref):
    nslab = fm_ref.shape[0]

    @pl.when(pl.program_id(1) == 0)
    def _():
        xb = x_ref[...].astype(BF16)
        xb_ref[...] = xb
        tok_ref[...] = jnp.dot(xb, wtok_ref[...], preferred_element_type=F32).astype(BF16)
        iw = lax.dot_general(wiw_ref[...], xb, NT_DIMS, preferred_element_type=F32)
        for s in range(nslab):
            iw_ref[s] = iw[:, s * TB:(s + 1) * TB]

    r = lax.dot_general(wt_ref[...], xb_ref[...], NT_DIMS, preferred_element_type=F32)
    for s in range(nslab):
        fm_ref[s] = r[:, s * TB:(s + 1) * TB].astype(BF16)


def _in_proj(x2d, wt_fm, w_tok, wt_iw, tm, tn):
    T, D = x2d.shape
    nslab = tm // TB
    return pl.pallas_call(
        _in_proj_kernel,
        grid=(T // tm, FM_ROWS // tn),
        in_specs=[
            pl.BlockSpec((tm, D), lambda i, j: (i, 0)),
            pl.BlockSpec((tn, D), lambda i, j: (j, 0)),
            pl.BlockSpec((D, TOK_COLS), lambda i, j: (0, 0)),
            pl.BlockSpec((N_IDX_HEADS, D), lambda i, j: (0, 0)),
        ],
        out_specs=[
            pl.BlockSpec((nslab, tn, TB), lambda i, j: (i, j, 0)),
            pl.BlockSpec((tm, TOK_COLS), lambda i, j: (i, 0)),
            pl.BlockSpec((nslab, N_IDX_HEADS, TB), lambda i, j: (i, 0, 0)),
        ],
        out_shape=[
            jax.ShapeDtypeStruct((T // TB, FM_ROWS, TB), BF16),
            jax.ShapeDtypeStruct((T, TOK_COLS), BF16),
            jax.ShapeDtypeStruct((T // TB, N_IDX_HEADS, TB), F32),
        ],
        scratch_shapes=[pltpu.VMEM((tm, D), BF16)],
        compiler_params=_params(("parallel", "arbitrary")),
        name="in_proj",
    )(x2d, wt_fm, w_tok, wt_iw)


def _rel_pos():
    kk = lax.broadcasted_iota(jnp.int32, (TB, TB), 0)
    qq = lax.broadcasted_iota(jnp.int32, (TB, TB), 1)
    return (qq - kk).astype(F32)


def _softmax_reset(m_ref, l_ref, acc_ref):
    m_ref[...] = jnp.full(m_ref.shape, NEG_BIG, F32)
    l_ref[...] = jnp.zeros(l_ref.shape, F32)
    acc_ref[...] = jnp.zeros(acc_ref.shape, F32)


def _diff_attn_kernel(lam_init, slopes_ref, lamp_ref, qT_ref, k_ref, vT_ref, g_ref, o_ref,
                      qm_ref, bias_ref, m_ref, l_ref, acc_ref, s_ref, p_ref, alpha_ref):
    qi = pl.program_id(1)
    lp = lamp_ref[...]
    lam = (jnp.exp(jnp.sum(lp[0:1] * lp[1:2], axis=1, keepdims=True))
           - jnp.exp(jnp.sum(lp[2:3] * lp[3:4], axis=1, keepdims=True)) + lam_init)
    rel = _rel_pos()
    row = lax.broadcasted_iota(jnp.int32, (DIFF_HEAD_DIM, TB), 0)
    hd = lambda h: slice(h * DIFF_HEAD_DIM, (h + 1) * DIFF_HEAD_DIM)

    _softmax_reset(m_ref, l_ref, acc_ref)
    rel2 = jnp.concatenate([rel, rel], axis=1)
    for h in range(N_DIFF_HEADS):
        qT = qT_ref[0, hd(h), :]
        zero = jnp.zeros_like(qT)
        qm_ref[h] = jnp.concatenate(
            [jnp.where(row < DIFF_QK_HALF, qT, zero), jnp.where(row >= DIFF_QK_HALF, qT, zero)], axis=1)
        bias_ref[h] = slopes_ref[h] * rel2

    def step(kj, diag):
        rows = pl.ds(pl.multiple_of(kj * TB, TB), TB)
        dist0 = ((qi - kj) * TB).astype(F32)
        for h in range(N_DIFF_HEADS):
            s_ref[h] = jnp.dot(k_ref[rows, hd(h)], qm_ref[h], preferred_element_type=F32)
        for h in range(N_DIFF_HEADS):
            s = s_ref[h] - bias_ref[h]
            if diag:
                s = jnp.where(rel2 >= 0, s, NEG_BIG)
            shift = slopes_ref[h] * dist0
            m_old = m_ref[h]
            m_new = jnp.maximum(m_old, jnp.max(s, axis=0, keepdims=True) - shift)
            alpha = jnp.exp2(m_old - m_new)
            p = jnp.exp2(s - (m_new + shift))
            l_ref[h] = alpha * l_ref[h] + jnp.sum(p, axis=0, keepdims=True)
            p_ref[h] = p.astype(BF16)
            alpha_ref[h] = alpha
            m_ref[h] = m_new
        for h in range(N_DIFF_HEADS):
            acc_ref[h] = alpha_ref[h] * acc_ref[h] + jnp.dot(vT_ref[kj, hd(h), :], p_ref[h],
                                                               preferred_element_type=F32)

    def body(kj, c):
        step(kj, False)
        return c

    lax.fori_loop(0, qi, body, 0)
    step(qi, True)
    for h in range(N_DIFF_HEADS):
        w = acc_ref[h] / l_ref[h]
        out = w[:, :TB] - lam * w[:, TB:]
        ms = jnp.mean(out * out, axis=0, keepdims=True)
        o_ref[0, hd(h), :] = (out * lax.rsqrt(ms + RMS_EPS) * g_ref[...]).astype(BF16)


def _diff_attn(fm, tok, slopes, lam_params, g_col, lam_init, B, S):
    nq = S // TB
    T = B * S
    return pl.pallas_call(
        functools.partial(_diff_attn_kernel, lam_init),
        grid=(B, nq),
        in_specs=[
            pl.BlockSpec(memory_space=pltpu.SMEM),
            pl.BlockSpec((4, DIFF_QK_HALF), lambda b, q: (0, 0)),
            pl.BlockSpec((1, DIFF_W, TB), lambda b, q: (b * nq + q, 0, 0)),
            pl.BlockSpec((S, DIFF_W), lambda b, q: (b, 0)),
            pl.BlockSpec((nq, DIFF_W, TB), lambda b, q: (b, 1, 0)),
            pl.BlockSpec((DIFF_HEAD_DIM, 1), lambda b, q: (0, 0)),
        ],
        out_specs=pl.BlockSpec((1, DIFF_W, TB), lambda b, q: (b * nq + q, 0, 0)),
        out_shape=jax.ShapeDtypeStruct((T // TB, DIFF_W, TB), BF16),
        scratch_shapes=[
            pltpu.VMEM((N_DIFF_HEADS, DIFF_HEAD_DIM, 2 * TB), BF16),
            pltpu.VMEM((N_DIFF_HEADS, TB, 2 * TB), F32),
            pltpu.VMEM((N_DIFF_HEADS, 1, 2 * TB), F32),
            pltpu.VMEM((N_DIFF_HEADS, 1, 2 * TB), F32),
            pltpu.VMEM((N_DIFF_HEADS, DIFF_HEAD_DIM, 2 * TB), F32),
            pltpu.VMEM((N_DIFF_HEADS, TB, 2 * TB), F32),
            pltpu.VMEM((N_DIFF_HEADS, TB, 2 * TB), BF16),
            pltpu.VMEM((N_DIFF_HEADS, 1, 2 * TB), F32),
        ],
        compiler_params=_params(("parallel", "arbitrary")),
        name="diff_attn",
    )(slopes, lam_params, fm, tok, fm, g_col)


def _dsa_attn_kernel(topk, slopes_ref, sqT_ref, iqT_ref, svT_ref, sk_ref, ikk_ref, iwT_ref, o_ref, keys_ref,
                     iqall_ref, bias_ref, qall_ref, m_ref, l_ref, acc_ref):
    qi = pl.program_id(1)
    rel = _rel_pos()
    iw = iwT_ref[0]
    row = lax.broadcasted_iota(jnp.int32, (2 * IDX_HEAD_DIM, TB), 0)

    for p in range(N_IDX_HEADS // 2):
        pair = iqT_ref[0, p * 2 * IDX_HEAD_DIM:(p + 1) * 2 * IDX_HEAD_DIM, :]
        zero = jnp.zeros_like(pair)
        iqall_ref[:, (2 * p) * TB:(2 * p + 1) * TB] = jnp.where(row < IDX_HEAD_DIM, pair, zero)
        iqall_ref[:, (2 * p + 1) * TB:(2 * p + 2) * TB] = jnp.where(row >= IDX_HEAD_DIM, pair, zero)

    def score_block(kj, diag):
        rows = pl.ds(pl.multiple_of(kj * TB, TB), TB)
        ikk = ikk_ref[rows, :]
        acc = jnp.zeros((TB, TB), F32)
        half = N_IDX_HEADS // 2
        for g in range(2):
            r = jnp.dot(ikk, iqall_ref[:, g * half * TB:(g + 1) * half * TB], preferred_element_type=F32)
            for j in range(half):
                hh = g * half + j
                acc = acc + iw[hh:hh + 1, :] * jnp.maximum(r[:, j * TB:(j + 1) * TB], 0.0)
        if diag:
            acc = jnp.where(rel >= 0, acc, -jnp.inf)
        bits = pltpu.bitcast(acc, jnp.int32)
        keys_ref[rows, :] = jnp.where(bits >= 0, bits, bits ^ jnp.int32(0x7FFFFFFF))

    def score_body(kj, c):
        score_block(kj, False)
        return c

    lax.fori_loop(0, qi, score_body, 0)
    score_block(qi, True)

    def count_ge(cand):
        def body(kj, cnt):
            blk = keys_ref[pl.ds(pl.multiple_of(kj * TB, TB), TB), :]
            return cnt + jnp.sum(jnp.where(blk >= cand, 1, 0).astype(jnp.int32), axis=0, keepdims=True)
        return lax.fori_loop(0, qi + 1, body, jnp.zeros((1, TB), jnp.int32))

    lo = jnp.where(count_ge(jnp.zeros((1, TB), jnp.int32)) >= topk, jnp.int32(0), jnp.int32(INT_MIN))

    def bit_body(t, lo):
        cand = lo + lax.shift_left(jnp.int32(1), jnp.int32(30) - t)
        return jnp.where(count_ge(cand) >= topk, cand, lo)

    lo = lax.fori_loop(0, 31, bit_body, lo)
    thr = jnp.maximum(lo, jnp.int32(KEY_NEG_INF + 1))

    hd = lambda h: slice(h * DSA_HEAD_DIM, (h + 1) * DSA_HEAD_DIM)
    lanes = lambda h: slice(h * TB, (h + 1) * TB)
    _softmax_reset(m_ref, l_ref, acc_ref)
    for h in range(N_DSA_HEADS):
        bias_ref[:, lanes(h)] = slopes_ref[h] * rel
        qall_ref[:, lanes(h)] = sqT_ref[0, hd(h), :]
    slope_row = jnp.concatenate([jnp.full((1, TB), slopes_ref[h], F32) for h in range(N_DSA_HEADS)], axis=1)

    def attend(kj, c):
        rows = pl.ds(pl.multiple_of(kj * TB, TB), TB)
        dist0 = ((qi - kj) * TB).astype(F32)
        s = jnp.dot(sk_ref[rows, :], qall_ref[...], preferred_element_type=F32) - bias_ref[...]
        sel = keys_ref[rows, :] >= thr
        s = jnp.where(jnp.concatenate([sel] * N_DSA_HEADS, axis=1), s, NEG_BIG)
        shift = slope_row * dist0
        m_old = m_ref[0]
        m_new = jnp.maximum(m_old, jnp.max(s, axis=0, keepdims=True) - shift)
        m_ref[0] = m_new
        alpha = jnp.exp2(m_old - m_new)
        p = jnp.exp2(s - (m_new + shift))
        l_ref[0] = alpha * l_ref[0] + jnp.sum(p, axis=0, keepdims=True)
        acc_ref[0] = alpha * acc_ref[0] + jnp.dot(svT_ref[kj], p.astype(BF16), preferred_element_type=F32)
        return c

    lax.fori_loop(0, qi + 1, attend, 0)
    out = acc_ref[0] / l_ref[0]
    for h in range(N_DSA_HEADS):
        o_ref[0, hd(h), :] = out[:, lanes(h)].astype(BF16)


def _dsa_attn(fm, tok, iwT, slopes, B, S):
    nq = S // TB
    T = B * S
    topk = min(DSA_TOPK_MAX, S // 4)
    sq_blk = (2 * DIFF_W) // DSA_QW
    iq_blk = (2 * DIFF_W + DSA_QW) // IDX_QW
    sv_blk = (2 * DIFF_W + DSA_QW + IDX_QW) // DSA_HEAD_DIM
    sk_blk = DIFF_W // DSA_HEAD_DIM
    return pl.pallas_call(
        functools.partial(_dsa_attn_kernel, topk),
        grid=(B, nq),
        in_specs=[
            pl.BlockSpec(memory_space=pltpu.SMEM),
            pl.BlockSpec((1, DSA_QW, TB), lambda b, q: (b * nq + q, sq_blk, 0)),
            pl.BlockSpec((1, IDX_QW, TB), lambda b, q: (b * nq + q, iq_blk, 0)),
            pl.BlockSpec((nq, DSA_HEAD_DIM, TB), lambda b, q: (b, sv_blk, 0)),
            pl.BlockSpec((S, DSA_HEAD_DIM), lambda b, q: (b, sk_blk)),
            pl.BlockSpec((S, 2 * IDX_HEAD_DIM), lambda b, q: (b, sk_blk + 1)),
            pl.BlockSpec((1, N_IDX_HEADS, TB), lambda b, q: (b * nq + q, 0, 0)),
        ],
        out_specs=pl.BlockSpec((1, DSA_QW, TB), lambda b, q: (b * nq + q, 0, 0)),
        out_shape=jax.ShapeDtypeStruct((T // TB, DSA_QW, TB), BF16),
        scratch_shapes=[
            pltpu.VMEM((S, TB), jnp.int32),
            pltpu.VMEM((2 * IDX_HEAD_DIM, N_IDX_HEADS * TB), BF16),
            pltpu.VMEM((TB, N_DSA_HEADS * TB), F32),
            pltpu.VMEM((DSA_HEAD_DIM, N_DSA_HEADS * TB), BF16),
            pltpu.VMEM((1, 1, N_DSA_HEADS * TB), F32),
            pltpu.VMEM((1, 1, N_DSA_HEADS * TB), F32),
            pltpu.VMEM((1, DSA_HEAD_DIM, N_DSA_HEADS * TB), F32),
        ],
        compiler_params=_params(("parallel", "arbitrary")),
        name="dsa_attn",
    )(slopes, fm, fm, fm, tok, tok, iwT)


def _layer_norm_cols(y, g, b):
    mu = jnp.mean(y, axis=0, keepdims=True)
    d = y - mu
    var = jnp.mean(d * d, axis=0, keepdims=True)
    return d * lax.rsqrt(var + LN_EPS) * g + b


def _out_proj_kernel(alpha, diffT_ref, dsaT_ref, woT_ref, x_ref, g_ref, b_ref, x1T_ref, x1Tb_ref):
    nslab = diffT_ref.shape[0]
    attnT = jnp.concatenate(
        [jnp.concatenate([diffT_ref[s], dsaT_ref[s]], axis=0) for s in range(nslab)], axis=1)
    mixedT = jnp.dot(woT_ref[...], attnT, preferred_element_type=F32)
    for s in range(nslab):
        xT = x_ref[s * TB:(s + 1) * TB, :].T
        y = _layer_norm_cols(alpha * xT + mixedT[:, s * TB:(s + 1) * TB], g_ref[...], b_ref[...])
        x1T_ref[s] = y
        x1Tb_ref[s] = y.astype(BF16)


def _out_proj(diffT, dsaT, woT, x2d, g, b, alpha, tm):
    T, D = x2d.shape
    nslab = tm // TB
    fm = lambda i: (i, 0, 0)
    return pl.pallas_call(
        functools.partial(_out_proj_kernel, alpha),
        grid=(T // tm,),
        in_specs=[
            pl.BlockSpec((nslab, DIFF_W, TB), fm),
            pl.BlockSpec((nslab, DSA_QW, TB), fm),
            pl.BlockSpec((D, DIFF_W + DSA_QW), lambda i: (0, 0)),
            pl.BlockSpec((tm, D), lambda i: (i, 0)),
            pl.BlockSpec((D, 1), lambda i: (0, 0)),
            pl.BlockSpec((D, 1), lambda i: (0, 0)),
        ],
        out_specs=[pl.BlockSpec((nslab, D, TB), fm), pl.BlockSpec((nslab, D, TB), fm)],
        out_shape=[jax.ShapeDtypeStruct((T // TB, D, TB), F32), jax.ShapeDtypeStruct((T // TB, D, TB), BF16)],
        compiler_params=_params(("parallel",)),
        name="out_proj_ln1",
    )(diffT, dsaT, woT, x2d, g, b)


def _top_values(s, k):
    vals = []
    for _ in range(k):
        m = jnp.max(s, axis=0, keepdims=True)
        vals.append(m)
        s = jnp.where(s == m, -jnp.inf, s)
    return vals


def _top_values_ranked(s, k):
    vals = []
    rank = jnp.full(s.shape, float(k), F32)
    for b in range(k):
        m = jnp.max(s, axis=0, keepdims=True)
        vals.append(m)
        hit = s == m
        s = jnp.where(hit, -jnp.inf, s)
        rank = jnp.where(hit, float(b), rank)
    return vals, rank


def _bf16_pair_words(x):
    bits = pltpu.bitcast(x, jnp.uint32)
    bits = bits + jnp.uint32(0x7FFF) + (lax.shift_right_logical(bits, jnp.uint32(16)) & jnp.uint32(1))
    hi = bits & jnp.uint32(0xFFFF0000)
    return hi | lax.shift_right_logical(hi, jnp.uint32(16))


def _peer_gate_kernel(x1Tb_ref, wqT_ref, k1_ref, k2_ref, cnt_ref, rnk_ref, e1_ref, e2_ref):
    nslab = x1Tb_ref.shape[0]
    x1cat = jnp.concatenate([x1Tb_ref[s] for s in range(nslab)], axis=1)
    qT = jnp.dot(wqT_ref[...], x1cat, preferred_element_type=F32).astype(BF16)
    for h in range(PEER_HEADS):
        base = h * PEER_QDIM
        s1w = jnp.dot(k1_ref[...], qT[base:base + PEER_HALF], preferred_element_type=F32)
        s2w = jnp.dot(k2_ref[...], qT[base + PEER_HALF:base + PEER_QDIM], preferred_element_type=F32)
        for s in range(nslab):
            s1 = s1w[:, s * TB:(s + 1) * TB]
            s2 = s2w[:, s * TB:(s + 1) * TB]
            v1 = _top_values(s1, PEER_TOPK)
            v2, rank2 = _top_values_ranked(s2, PEER_TOPK)
            v1c = jnp.concatenate(v1, axis=0)
            v2c = jnp.concatenate(v2, axis=0)
            half = PEER_TOPK // 2
            cand = jnp.concatenate(
                [v1[0] + v2c] + [v1[a] + v2c[:half] for a in range(1, half)] + [v1c[half:] + v2[0]], axis=0)
            tau = _top_values(cand, PEER_TOPK)[-1]
            m1, m2 = v1[0], v2[0]
            z = jnp.sum(jnp.where(cand >= tau, jnp.exp(cand - (m1 + m2)), 0.0), axis=0, keepdims=True)
            cnt = jnp.zeros(s1.shape, F32)
            for vb in v2:
                cnt = cnt + jnp.where(s1 + vb >= tau, 1.0, 0.0)
            rows = slice(h * PEER_NKEYS, (h + 1) * PEER_NKEYS)
            cnt_ref[s, rows, :] = _bf16_pair_words(cnt)
            rnk_ref[s, rows, :] = rank2.astype(BF16)
            e1_ref[s, rows, :] = _bf16_pair_words(jnp.exp(s1 - m1) / z)
            e2_ref[s, rows, :] = jnp.exp(s2 - m2).astype(BF16)


def _peer_gate(x1Tb, wqT, k1, k2, tm):
    nblk, D, _ = x1Tb.shape
    W = PEER_HEADS * PEER_NKEYS
    nslab = tm // TB
    fm = lambda i: (i, 0, 0)
    return pl.pallas_call(
        _peer_gate_kernel,
        grid=(nblk // nslab,),
        in_specs=[
            pl.BlockSpec((nslab, D, TB), fm),
            pl.BlockSpec((PEER_HEADS * PEER_QDIM, D), lambda i: (0, 0)),
            pl.BlockSpec((PEER_NKEYS, PEER_HALF), lambda i: (0, 0)),
            pl.BlockSpec((PEER_NKEYS, PEER_HALF), lambda i: (0, 0)),
        ],
        out_specs=[pl.BlockSpec((nslab, W, TB), fm)] * 4,
        out_shape=[jax.ShapeDtypeStruct((nblk, W, TB), dt) for dt in (jnp.uint32, BF16, jnp.uint32, BF16)],
        compiler_params=_params(("parallel",)),
        name="peer_gate",
    )(x1Tb, wqT, k1, k2)


def _peer_gate_units(i0, cnt_ref, rnk_ref, e1_ref, e2_ref, g_ref):
    def pair_row(ref, s, r):
        return pltpu.bitcast(jnp.broadcast_to(ref[s, r, :], (PEER_NKEYS // 2, TB)), BF16)

    def unit(ii, s):
        def run():
            i = jnp.minimum(i0 + ii, PEER_NKEYS - 1)
            gate = jnp.zeros((PEER_NKEYS, TB), BF16)
            for hd in range(PEER_HEADS):
                r = pl.ds(hd * PEER_NKEYS + i, 1)
                keys = slice(hd * PEER_NKEYS, (hd + 1) * PEER_NKEYS)
                e2 = e2_ref[s, keys, :]
                sel = jnp.where(rnk_ref[s, keys, :] < pair_row(cnt_ref, s, r), e2, jnp.zeros_like(e2))
                gate = gate + pair_row(e1_ref, s, r) * sel
            g_ref[ii * PEER_NKEYS:(ii + 1) * PEER_NKEYS, s * TB:(s + 1) * TB] = gate
        return run
    return [unit(ii, s) for ii in range(g_ref.shape[0] // PEER_NKEYS) for s in range(cnt_ref.shape[0])]


def _peer_dense_kernel(x1Tb_ref, u_ref, vT_ref, cnt_ref, rnk_ref, e1_ref, e2_ref, yT_ref,
                       ga_ref, gb_ref, a_ref):
    ei = pl.program_id(1)
    te, tm = ga_ref.shape
    n_i = te // PEER_NKEYS
    nslab = cnt_ref.shape[0]
    gate_args = (cnt_ref, rnk_ref, e1_ref, e2_ref)

    @pl.when(ei == 0)
    def _():
        yT_ref[...] = jnp.zeros_like(yT_ref)
        for unit in _peer_gate_units(0, *gate_args, ga_ref):
            unit()

    x1cat = jnp.concatenate([x1Tb_ref[s] for s in range(nslab)], axis=1)

    def activations(t, g_ref):
        hT = jnp.dot(u_ref[t * te:(t + 1) * te, :], x1cat, preferred_element_type=F32)
        act = 0.5 * hT * (1.0 + lax.erf(hT * np.float32(1.0 / math.sqrt(2.0))))
        a_ref[t * te:(t + 1) * te, :] = (act * g_ref[...].astype(F32)).astype(BF16)

    for unit in _peer_gate_units((2 * ei + 1) * n_i, *gate_args, gb_ref):
        unit()
    activations(0, ga_ref)
    for unit in _peer_gate_units((2 * ei + 2) * n_i, *gate_args, ga_ref):
        unit()
    activations(1, gb_ref)
    y = jnp.dot(vT_ref[...], a_ref[...], preferred_element_type=F32)
    for s in range(nslab):
        yT_ref[s] += y[:, s * TB:(s + 1) * TB]


def _peer_dense(x1Tb, u, vT, gates, tm, te):
    nblk, D, _ = x1Tb.shape
    E = u.shape[0]
    W = PEER_HEADS * PEER_NKEYS
    nslab = tm // TB
    fm = lambda i, e: (i, 0, 0)
    return pl.pallas_call(
        _peer_dense_kernel,
        grid=(nblk // nslab, E // (2 * te)),
        in_specs=[
            pl.BlockSpec((nslab, D, TB), fm),
            pl.BlockSpec((2 * te, D), lambda i, e: (e, 0)),
            pl.BlockSpec((D, 2 * te), lambda i, e: (0, e)),
            pl.BlockSpec((nslab, W, TB), fm),
            pl.BlockSpec((nslab, W, TB), fm),
            pl.BlockSpec((nslab, W, TB), fm),
            pl.BlockSpec((nslab, W, TB), fm),
        ],
        out_specs=pl.BlockSpec((nslab, D, TB), fm),
        out_shape=jax.ShapeDtypeStruct((nblk, D, TB), F32),
        scratch_shapes=[pltpu.VMEM((te, tm), BF16), pltpu.VMEM((te, tm), BF16),
                        pltpu.VMEM((2 * te, tm), BF16)],
        compiler_params=_params(("parallel", "arbitrary")),
        name="peer_dense",
    )(x1Tb, u, vT, *gates)


def _ln2_kernel(alpha, x1T_ref, yT_ref, g_ref, b_ref, o_ref):
    for s in range(x1T_ref.shape[0]):
        z = _layer_norm_cols(alpha * x1T_ref[s] + yT_ref[s], g_ref[...], b_ref[...])
        o_ref[s * TB:(s + 1) * TB, :] = z.T


def _ln2(x1T, yT, g, b, alpha, tm):
    nblk, D, _ = x1T.shape
    nslab = tm // TB
    fm = lambda i: (i, 0, 0)
    return pl.pallas_call(
        functools.partial(_ln2_kernel, alpha),
        grid=(nblk // nslab,),
        in_specs=[
            pl.BlockSpec((nslab, D, TB), fm),
            pl.BlockSpec((nslab, D, TB), fm),
            pl.BlockSpec((D, 1), lambda i: (0, 0)),
            pl.BlockSpec((D, 1), lambda i: (0, 0)),
        ],
        out_specs=pl.BlockSpec((tm, D), lambda i: (i, 0)),
        out_shape=jax.ShapeDtypeStruct((nblk * TB, D), F32),
        compiler_params=_params(("parallel",)),
        name="ln2_out",
    )(x1T, yT, g, b)


def _tiles(T):
    tm = 2 * TB if T % (2 * TB) == 0 else TB
    return dict(proj_tm=tm, proj_tn=FM_ROWS // 3, outproj_tm=tm, dense_tm=tm, dense_te=4 * PEER_NKEYS)


def kernel(x, w_in, w_o, lambda_q1, lambda_k1, lambda_q2, lambda_k2, subln_g, ln1_g, ln1_b,
           peer_wq, peer_k1, peer_k2, peer_u, peer_v, ln2_g, ln2_b):
    B, S, D = x.shape
    T = B * S
    depth = w_in.shape[0]
    assert S % TB == 0 and w_in.shape[2] == sum(IN_SPLITS)
    alpha = float((2 * depth) ** 0.25)
    slopes_diff, slopes_dsa = (s * F32(LOG2E) for s in _alibi_slopes())
    tiles = _tiles(T)
    offs = np.cumsum((0,) + IN_SPLITS)
    col = lambda w, k: w[:, offs[k]:offs[k + 1]]

    xt = x.reshape(T, D)
    for l in range(depth):
        w = w_in[l]
        dq, dk, dv, sq, sk, sv, iq, ik, iw = (col(w, k) for k in range(9))
        wt_fm = jnp.concatenate(
            [dq * (LOG2E * DIFF_QK_HALF ** -0.5), dv, sq * (LOG2E * DSA_HEAD_DIM ** -0.5), iq, sv],
            axis=1).T.astype(BF16)
        w_tok = jnp.concatenate([dk, sk, ik, ik], axis=1).astype(BF16)
        wt_iw = (iw * ((IDX_HEAD_DIM ** -0.5) * (N_IDX_HEADS ** -0.5))).T.astype(BF16)
        fm, tok, iwT = _in_proj(xt, wt_fm, w_tok, wt_iw, tiles["proj_tm"], tiles["proj_tn"])

        lam_init = 0.8 - 0.6 * math.exp(-0.3 * l)
        lam_params = jnp.stack([lambda_q1[l], lambda_k1[l], lambda_q2[l], lambda_k2[l]]).astype(F32)
        g_col = (subln_g[l].astype(F32) * (1.0 - lam_init)).reshape(DIFF_HEAD_DIM, 1)
        diffT = _diff_attn(fm, tok, slopes_diff, lam_params, g_col, lam_init, B, S)
        dsaT = _dsa_attn(fm, tok, iwT, slopes_dsa, B, S)

        colv = lambda p: p.reshape(D, 1).astype(F32)
        x1T, x1Tb = _out_proj(diffT, dsaT, w_o[l].T.astype(BF16), xt, colv(ln1_g[l]), colv(ln1_b[l]), alpha,
                              tiles["outproj_tm"])

        gates = _peer_gate(x1Tb, peer_wq[l].T.astype(BF16), peer_k1[l].astype(BF16), peer_k2[l].astype(BF16),
                           tiles["outproj_tm"])
        yT = _peer_dense(x1Tb, peer_u[l].astype(BF16), peer_v[l].T.astype(BF16), gates,
                         tiles["dense_tm"], tiles["dense_te"])
        xt = _ln2(x1T, yT, colv(ln2_g[l]), colv(ln2_b[l]), alpha, tiles["outproj_tm"])
    return xt.reshape(B, S, D)
```

```python
import functools
import math

import jax
import jax.numpy as jnp
import numpy as np
from jax import lax
from jax.experimental import pallas as pl
from jax.experimental.pallas import tpu as pltpu

F32 = jnp.float32
BF16 = jnp.bfloat16

N_DIFF_HEADS = 8
DIFF_HEAD_DIM = 128
DIFF_QK_HALF = DIFF_HEAD_DIM // 2
N_DSA_HEADS = 8
DSA_HEAD_DIM = 128
N_IDX_HEADS = 16
IDX_HEAD_DIM = 64
DSA_TOPK_MAX = 256
PEER_HEADS = 8
PEER_NKEYS = 128
PEER_QDIM = 256
PEER_HALF = PEER_QDIM // 2
PEER_TOPK = 16
LN_EPS = 1e-5
RMS_EPS = 1e-5

DIFF_W = N_DIFF_HEADS * DIFF_HEAD_DIM
DSA_QW = N_DSA_HEADS * DSA_HEAD_DIM
IDX_QW = N_IDX_HEADS * IDX_HEAD_DIM
IN_SPLITS = (DIFF_W, DIFF_W, DIFF_W, DSA_QW, DSA_HEAD_DIM, DSA_HEAD_DIM, IDX_QW, IDX_HEAD_DIM, N_IDX_HEADS)

TB = 256
FM_ROWS = DIFF_W * 2 + DSA_QW + IDX_QW + DSA_HEAD_DIM
TOK_COLS = DIFF_W + DSA_HEAD_DIM + 2 * IDX_HEAD_DIM
VMEM_LIMIT = 56 * 1024 * 1024
DENSE_H_CHUNKS = 4
DENSE_Y_CHUNKS = 4

LOG2E = math.log2(math.e)
NEG_BIG = -1e30
KEY_NEG_INF = (0xFF800000 ^ 0x7FFFFFFF) - 2 ** 32
INT_MIN = -(2 ** 31)

NT_DIMS = (((1,), (1,)), ((), ()))


def _alibi_slopes():
    n = N_DIFF_HEADS + N_DSA_HEADS
    s = 2.0 ** (-8.0 * np.arange(1, n + 1) / n)
    return (jnp.asarray(s[0::2], dtype=F32), jnp.asarray(s[1::2], dtype=F32))


def _params(sem):
    return pltpu.CompilerParams(dimension_semantics=sem, vmem_limit_bytes=VMEM_LIMIT)


def _in_proj_kernel(x_ref, wt_ref, wtok_ref, wiw_ref, fm_ref, tok_ref, iw_ref, xb_ref):
    nslab = fm_ref.shape[0]

    @pl.when(pl.program_id(1) == 0)
    def _():
        xb = x_ref[...].astype(BF16)
        xb_ref[...] = xb
        tok_ref[...] = jnp.dot(xb, wtok_ref[...], preferred_element_type=F32).astype(BF16)
        iw = lax.dot_general(wiw_ref[...], xb, NT_DIMS, preferred_element_type=F32)
        for s in range(nslab):
            iw_ref[s] = iw[:, s * TB:(s + 1) * TB]

    r = lax.dot_general(wt_ref[...], xb_ref[...], NT_DIMS, preferred_element_type=F32)
    for s in range(nslab):
        fm_ref[s] = r[:, s * TB:(s + 1) * TB].astype(BF16)


def _in_proj(x2d, wt_fm, w_tok, wt_iw, tm, tn):
    T, D = x2d.shape
    nslab = tm // TB
    return pl.pallas_call(
        _in_proj_kernel,
        grid=(T // tm, FM_ROWS // tn),
        in_specs=[
            pl.BlockSpec((tm, D), lambda i, j: (i, 0)),
            pl.BlockSpec((tn, D), lambda i, j: (j, 0)),
            pl.BlockSpec((D, TOK_COLS), lambda i, j: (0, 0)),
            pl.BlockSpec((N_IDX_HEADS, D), lambda i, j: (0, 0)),
        ],
        out_specs=[
            pl.BlockSpec((nslab, tn, TB), lambda i, j: (i, j, 0)),
            pl.BlockSpec((tm, TOK_COLS), lambda i, j: (i, 0)),
            pl.BlockSpec((nslab, N_IDX_HEADS, TB), lambda i, j: (i, 0, 0)),
        ],
        out_shape=[
            jax.ShapeDtypeStruct((T // TB, FM_ROWS, TB), BF16),
            jax.ShapeDtypeStruct((T, TOK_COLS), BF16),
            jax.ShapeDtypeStruct((T // TB, N_IDX_HEADS, TB), F32),
        ],
        scratch_shapes=[pltpu.VMEM((tm, D), BF16)],
        compiler_params=_params(("parallel", "arbitrary")),
        name="in_proj",
    )(x2d, wt_fm, w_tok, wt_iw)


def _rel_pos():
    kk = lax.broadcasted_iota(jnp.int32, (TB, TB), 0)
    qq = lax.broadcasted_iota(jnp.int32, (TB, TB), 1)
    return (qq - kk).astype(F32)


def _softmax_reset(m_ref, l_ref, acc_ref):
    m_ref[...] = jnp.full(m_ref.shape, NEG_BIG, F32)
    l_ref[...] = jnp.zeros(l_ref.shape, F32)
    acc_ref[...] = jnp.zeros(acc_ref.shape, F32)


def _diff_attn_kernel(lam_init, slopes_ref, lamp_ref, qT_ref, k_ref, vT_ref, g_ref, o_ref,
                      qm_ref, bias_ref, m_ref, l_ref, acc_ref, s_ref, p_ref, alpha_ref):
    qi = pl.program_id(1)
    lp = lamp_ref[...]
    lam = (jnp.exp(jnp.sum(lp[0:1] * lp[1:2], axis=1, keepdims=True))
           - jnp.exp(jnp.sum(lp[2:3] * lp[3:4], axis=1, keepdims=True)) + lam_init)
    rel = _rel_pos()
    row = lax.broadcasted_iota(jnp.int32, (DIFF_HEAD_DIM, TB), 0)
    hd = lambda h: slice(h * DIFF_HEAD_DIM, (h + 1) * DIFF_HEAD_DIM)

    _softmax_reset(m_ref, l_ref, acc_ref)
    rel2 = jnp.concatenate([rel, rel], axis=1)
    for h in range(N_DIFF_HEADS):
        qT = qT_ref[0, hd(h), :]
        zero = jnp.zeros_like(qT)
        qm_ref[h] = jnp.concatenate(
            [jnp.where(row < DIFF_QK_HALF, qT, zero), jnp.where(row >= DIFF_QK_HALF, qT, zero)], axis=1)
        bias_ref[h] = slopes_ref[h] * rel2

    def step(kj, diag):
        rows = pl.ds(pl.multiple_of(kj * TB, TB), TB)
        dist0 = ((qi - kj) * TB).astype(F32)
        for h in range(N_DIFF_HEADS):
            s_ref[h] = jnp.dot(k_ref[rows, hd(h)], qm_ref[h], preferred_element_type=F32)
        for h in range(N_DIFF_HEADS):
            s = s_ref[h] - bias_ref[h]
            if diag:
                s = jnp.where(rel2 >= 0, s, NEG_BIG)
            shift = slopes_ref[h] * dist0
            m_old = m_ref[h]
            m_new = jnp.maximum(m_old, jnp.max(s, axis=0, keepdims=True) - shift)
            alpha = jnp.exp2(m_old - m_new)
            p = jnp.exp2(s - (m_new + shift))
            l_ref[h] = alpha * l_ref[h] + jnp.sum(p, axis=0, keepdims=True)
            p_ref[h] = p.astype(BF16)
            alpha_ref[h] = alpha
            m_ref[h] = m_new
        for h in range(N_DIFF_HEADS):
            acc_ref[h] = alpha_ref[h] * acc_ref[h] + jnp.dot(vT_ref[kj, hd(h), :], p_ref[h],
                                                               preferred_element_type=F32)

    def body(kj, c):
        step(kj, False)
        return c

    lax.fori_loop(0, qi, body, 0)
    step(qi, True)
    for h in range(N_DIFF_HEADS):
        w = acc_ref[h] / l_ref[h]
        out = w[:, :TB] - lam * w[:, TB:]
        ms = jnp.mean(out * out, axis=0, keepdims=True)
        o_ref[0, hd(h), :] = (out * lax.rsqrt(ms + RMS_EPS) * g_ref[...]).astype(BF16)


def _diff_attn(fm, tok, slopes, lam_params, g_col, lam_init, B, S):
    nq = S // TB
    T = B * S
    return pl.pallas_call(
        functools.partial(_diff_attn_kernel, lam_init),
        grid=(B, nq),
        in_specs=[
            pl.BlockSpec(memory_space=pltpu.SMEM),
            pl.BlockSpec((4, DIFF_QK_HALF), lambda b, q: (0, 0)),
            pl.BlockSpec((1, DIFF_W, TB), lambda b, q: (b * nq + q, 0, 0)),
            pl.BlockSpec((S, DIFF_W), lambda b, q: (b, 0)),
            pl.BlockSpec((nq, DIFF_W, TB), lambda b, q: (b, 1, 0)),
            pl.BlockSpec((DIFF_HEAD_DIM, 1), lambda b, q: (0, 0)),
        ],
        out_specs=pl.BlockSpec((1, DIFF_W, TB), lambda b, q: (b * nq + q, 0, 0)),
        out_shape=jax.ShapeDtypeStruct((T // TB, DIFF_W, TB), BF16),
        scratch_shapes=[
            pltpu.VMEM((N_DIFF_HEADS, DIFF_HEAD_DIM, 2 * TB), BF16),
            pltpu.VMEM((N_DIFF_HEADS, TB, 2 * TB), F32),
            pltpu.VMEM((N_DIFF_HEADS, 1, 2 * TB), F32),
            pltpu.VMEM((N_DIFF_HEADS, 1, 2 * TB), F32),
            pltpu.VMEM((N_DIFF_HEADS, DIFF_HEAD_DIM, 2 * TB), F32),
            pltpu.VMEM((N_DIFF_HEADS, TB, 2 * TB), F32),
            pltpu.VMEM((N_DIFF_HEADS, TB, 2 * TB), BF16),
            pltpu.VMEM((N_DIFF_HEADS, 1, 2 * TB), F32),
        ],
        compiler_params=_params(("parallel", "arbitrary")),
        name="diff_attn",
    )(slopes, lam_params, fm, tok, fm, g_col)


def _dsa_attn_kernel(topk, slopes_ref, sqT_ref, iqT_ref, svT_ref, sk_ref, ikk_ref, iwT_ref, o_ref, keys_ref,
                     iqall_ref, bias_ref, qall_ref, m_ref, l_ref, acc_ref):
    qi = pl.program_id(1)
    rel = _rel_pos()
    iw = iwT_ref[0]
    row = lax.broadcasted_iota(jnp.int32, (2 * IDX_HEAD_DIM, TB), 0)

    for p in range(N_IDX_HEADS // 2):
        pair = iqT_ref[0, p * 2 * IDX_HEAD_DIM:(p + 1) * 2 * IDX_HEAD_DIM, :]
        zero = jnp.zeros_like(pair)
        iqall_ref[:, (2 * p) * TB:(2 * p + 1) * TB] = jnp.where(row < IDX_HEAD_DIM, pair, zero)
        iqall_ref[:, (2 * p + 1) * TB:(2 * p + 2) * TB] = jnp.where(row >= IDX_HEAD_DIM, pair, zero)

    def score_block(kj, diag):
        rows = pl.ds(pl.multiple_of(kj * TB, TB), TB)
        ikk = ikk_ref[rows, :]
        acc = jnp.zeros((TB, TB), F32)
        half = N_IDX_HEADS // 2
        for g in range(2):
            r = jnp.dot(ikk, iqall_ref[:, g * half * TB:(g + 1) * half * TB], preferred_element_type=F32)
            for j in range(half):
                hh = g * half + j
                acc = acc + iw[hh:hh + 1, :] * jnp.maximum(r[:, j * TB:(j + 1) * TB], 0.0)
        if diag:
            acc = jnp.where(rel >= 0, acc, -jnp.inf)
        bits = pltpu.bitcast(acc, jnp.int32)
        keys_ref[rows, :] = jnp.where(bits >= 0, bits, bits ^ jnp.int32(0x7FFFFFFF))

    def score_body(kj, c):
        score_block(kj, False)
        return c

    lax.fori_loop(0, qi, score_body, 0)
    score_block(qi, True)

    def count_ge(cand):
        def body(kj, cnt):
            blk = keys_ref[pl.ds(pl.multiple_of(kj * TB, TB), TB), :]
            return cnt + jnp.sum(jnp.where(blk >= cand, 1, 0).astype(jnp.int32), axis=0, keepdims=True)
        return lax.fori_loop(0, qi + 1, body, jnp.zeros((1, TB), jnp.int32))

    lo = jnp.where(count_ge(jnp.zeros((1, TB), jnp.int32)) >= topk, jnp.int32(0), jnp.int32(INT_MIN))

    def bit_body(t, lo):
        cand = lo + lax.shift_left(jnp.int32(1), jnp.int32(30) - t)
        return jnp.where(count_ge(cand) >= topk, cand, lo)

    lo = lax.fori_loop(0, 31, bit_body, lo)
    thr = jnp.maximum(lo, jnp.int32(KEY_NEG_INF + 1))

    hd = lambda h: slice(h * DSA_HEAD_DIM, (h + 1) * DSA_HEAD_DIM)
    lanes = lambda h: slice(h * TB, (h + 1) * TB)
    _softmax_reset(m_ref, l_ref, acc_ref)
    for h in range(N_DSA_HEADS):
        bias_ref[:, lanes(h)] = slopes_ref[h] * rel
        qall_ref[:, lanes(h)] = sqT_ref[0, hd(h), :]
    slope_row = jnp.concatenate([jnp.full((1, TB), slopes_ref[h], F32) for h in range(N_DSA_HEADS)], axis=1)

    def attend(kj, c):
        rows = pl.ds(pl.multiple_of(kj * TB, TB), TB)
        dist0 = ((qi - kj) * TB).astype(F32)
        s = jnp.dot(sk_ref[rows, :], qall_ref[...], preferred_element_type=F32) - bias_ref[...]
        sel = keys_ref[rows, :] >= thr
        s = jnp.where(jnp.concatenate([sel] * N_DSA_HEADS, axis=1), s, NEG_BIG)
        shift = slope_row * dist0
        m_old = m_ref[0]
        m_new = jnp.maximum(m_old, jnp.max(s, axis=0, keepdims=True) - shift)
        m_ref[0] = m_new
        alpha = jnp.exp2(m_old - m_new)
        p = jnp.exp2(s - (m_new + shift))
        l_ref[0] = alpha * l_ref[0] + jnp.sum(p, axis=0, keepdims=True)
        acc_ref[0] = alpha * acc_ref[0] + jnp.dot(svT_ref[kj], p.astype(BF16), preferred_element_type=F32)
        return c

    lax.fori_loop(0, qi + 1, attend, 0)
    out = acc_ref[0] / l_ref[0]
    for h in range(N_DSA_HEADS):
        o_ref[0, hd(h), :] = out[:, lanes(h)].astype(BF16)


def _dsa_attn(fm, tok, iwT, slopes, B, S):
    nq = S // TB
    T = B * S
    topk = min(DSA_TOPK_MAX, S // 4)
    sq_blk = (2 * DIFF_W) // DSA_QW
    iq_blk = (2 * DIFF_W + DSA_QW) // IDX_QW
    sv_blk = (2 * DIFF_W + DSA_QW + IDX_QW) // DSA_HEAD_DIM
    sk_blk = DIFF_W // DSA_HEAD_DIM
    return pl.pallas_call(
        functools.partial(_dsa_attn_kernel, topk),
        grid=(B, nq),
        in_specs=[
            pl.BlockSpec(memory_space=pltpu.SMEM),
            pl.BlockSpec((1, DSA_QW, TB), lambda b, q: (b * nq + q, sq_blk, 0)),
            pl.BlockSpec((1, IDX_QW, TB), lambda b, q: (b * nq + q, iq_blk, 0)),
            pl.BlockSpec((nq, DSA_HEAD_DIM, TB), lambda b, q: (b, sv_blk, 0)),
            pl.BlockSpec((S, DSA_HEAD_DIM), lambda b, q: (b, sk_blk)),
            pl.BlockSpec((S, 2 * IDX_HEAD_DIM), lambda b, q: (b, sk_blk + 1)),
            pl.BlockSpec((1, N_IDX_HEADS, TB), lambda b, q: (b * nq + q, 0, 0)),
        ],
        out_specs=pl.BlockSpec((1, DSA_QW, TB), lambda b, q: (b * nq + q, 0, 0)),
        out_shape=jax.ShapeDtypeStruct((T // TB, DSA_QW, TB), BF16),
        scratch_shapes=[
            pltpu.VMEM((S, TB), jnp.int32),
            pltpu.VMEM((2 * IDX_HEAD_DIM, N_IDX_HEADS * TB), BF16),
            pltpu.VMEM((TB, N_DSA_HEADS * TB), F32),
            pltpu.VMEM((DSA_HEAD_DIM, N_DSA_HEADS * TB), BF16),
            pltpu.VMEM((1, 1, N_DSA_HEADS * TB), F32),
            pltpu.VMEM((1, 1, N_DSA_HEADS * TB), F32),
            pltpu.VMEM((1, DSA_HEAD_DIM, N_DSA_HEADS * TB), F32),
        ],
        compiler_params=_params(("parallel", "arbitrary")),
        name="dsa_attn",
    )(slopes, fm, fm, fm, tok, tok, iwT)


def _layer_norm_cols(y, g, b):
    mu = jnp.mean(y, axis=0, keepdims=True)
    d = y - mu
    var = jnp.mean(d * d, axis=0, keepdims=True)
    return d * lax.rsqrt(var + LN_EPS) * g + b


def _out_proj_kernel(alpha, diffT_ref, dsaT_ref, woT_ref, x_ref, g_ref, b_ref, x1T_ref, x1Tb_ref):
    nslab = diffT_ref.shape[0]
    attnT = jnp.concatenate(
        [jnp.concatenate([diffT_ref[s], dsaT_ref[s]], axis=0) for s in range(nslab)], axis=1)
    mixedT = jnp.dot(woT_ref[...], attnT, preferred_element_type=F32)
    for s in range(nslab):
        xT = x_ref[s * TB:(s + 1) * TB, :].T
        y = _layer_norm_cols(alpha * xT + mixedT[:, s * TB:(s + 1) * TB], g_ref[...], b_ref[...])
        x1T_ref[s] = y
        x1Tb_ref[s] = y.astype(BF16)


def _out_proj(diffT, dsaT, woT, x2d, g, b, alpha, tm):
    T, D = x2d.shape
    nslab = tm // TB
    fm = lambda i: (i, 0, 0)
    return pl.pallas_call(
        functools.partial(_out_proj_kernel, alpha),
        grid=(T // tm,),
        in_specs=[
            pl.BlockSpec((nslab, DIFF_W, TB), fm),
            pl.BlockSpec((nslab, DSA_QW, TB), fm),
            pl.BlockSpec((D, DIFF_W + DSA_QW), lambda i: (0, 0)),
            pl.BlockSpec((tm, D), lambda i: (i, 0)),
            pl.BlockSpec((D, 1), lambda i: (0, 0)),
            pl.BlockSpec((D, 1), lambda i: (0, 0)),
        ],
        out_specs=[pl.BlockSpec((nslab, D, TB), fm), pl.BlockSpec((nslab, D, TB), fm)],
        out_shape=[jax.ShapeDtypeStruct((T // TB, D, TB), F32), jax.ShapeDtypeStruct((T // TB, D, TB), BF16)],
        compiler_params=_params(("parallel",)),
        name="out_proj_ln1",
    )(diffT, dsaT, woT, x2d, g, b)


def _top_values(s, k):
    vals = []
    for _ in range(k):
        m = jnp.max(s, axis=0, keepdims=True)
        vals.append(m)
        s = jnp.where(s == m, -jnp.inf, s)
    return vals


def _top_values_ranked(s, k):
    vals = []
    rank = jnp.full(s.shape, float(k), F32)
    for b in range(k):
        m = jnp.max(s, axis=0, keepdims=True)
        vals.append(m)
        hit = s == m
        s = jnp.where(hit, -jnp.inf, s)
        rank = jnp.where(hit, float(b), rank)
    return vals, rank


def _bf16_pair_words(x):
    bits = pltpu.bitcast(x, jnp.uint32)
    bits = bits + jnp.uint32(0x7FFF) + (lax.shift_right_logical(bits, jnp.uint32(16)) & jnp.uint32(1))
    hi = bits & jnp.uint32(0xFFFF0000)
    return hi | lax.shift_right_logical(hi, jnp.uint32(16))


def _peer_gate_kernel(x1Tb_ref, wqT_ref, k1_ref, k2_ref, cnt_ref, rnk_ref, e1_ref, e2_ref):
    nslab = x1Tb_ref.shape[0]
    x1cat = jnp.concatenate([x1Tb_ref[s] for s in range(nslab)], axis=1)
    qT = jnp.dot(wqT_ref[...], x1cat, preferred_element_type=F32).astype(BF16)
    for h in range(PEER_HEADS):
        base = h * PEER_QDIM
        s1w = jnp.dot(k1_ref[...], qT[base:base + PEER_HALF], preferred_element_type=F32)
        s2w = jnp.dot(k2_ref[...], qT[base + PEER_HALF:base + PEER_QDIM], preferred_element_type=F32)
        for s in range(nslab):
            s1 = s1w[:, s * TB:(s + 1) * TB]
            s2 = s2w[:, s * TB:(s + 1) * TB]
            v1 = _top_values(s1, PEER_TOPK)
            v2, rank2 = _top_values_ranked(s2, PEER_TOPK)
            v1c = jnp.concatenate(v1, axis=0)
            v2c = jnp.concatenate(v2, axis=0)
            half = PEER_TOPK // 2
            cand = jnp.concatenate(
                [v1[0] + v2c] + [v1[a] + v2c[:half] for a in range(1, half)] + [v1c[half:] + v2[0]], axis=0)
            tau = _top_values(cand, PEER_TOPK)[-1]
            m1, m2 = v1[0], v2[0]
            z = jnp.sum(jnp.where(cand >= tau, jnp.exp(cand - (m1 + m2)), 0.0), axis=0, keepdims=True)
            cnt = jnp.zeros(s1.shape, F32)
            for vb in v2:
                cnt = cnt + jnp.where(s1 + vb >= tau, 1.0, 0.0)
            rows = slice(h * PEER_NKEYS, (h + 1) * PEER_NKEYS)
            cnt_ref[s, rows, :] = _bf16_pair_words(cnt)
            rnk_ref[s, rows, :] = rank2.astype(BF16)
            e1_ref[s, rows, :] = _bf16_pair_words(jnp.exp(s1 - m1) / z)
            e2_ref[s, rows, :] = jnp.exp(s2 - m2).astype(BF16)


def _peer_gate(x1Tb, wqT, k1, k2, tm):
    nblk, D, _ = x1Tb.shape
    W = PEER_HEADS * PEER_NKEYS
    nslab = tm // TB
    fm = lambda i: (i, 0, 0)
    return pl.pallas_call(
        _peer_gate_kernel,
        grid=(nblk // nslab,),
        in_specs=[
            pl.BlockSpec((nslab, D, TB), fm),
            pl.BlockSpec((PEER_HEADS * PEER_QDIM, D), lambda i: (0, 0)),
            pl.BlockSpec((PEER_NKEYS, PEER_HALF), lambda i: (0, 0)),
            pl.BlockSpec((PEER_NKEYS, PEER_HALF), lambda i: (0, 0)),
        ],
        out_specs=[pl.BlockSpec((nslab, W, TB), fm)] * 4,
        out_shape=[jax.ShapeDtypeStruct((nblk, W, TB), dt) for dt in (jnp.uint32, BF16, jnp.uint32, BF16)],
        compiler_params=_params(("parallel",)),
        name="peer_gate",
    )(x1Tb, wqT, k1, k2)


def _peer_gate_units(i0, cnt_ref, rnk_ref, e1_ref, e2_ref, g_ref):
    def pair_row(ref, s, r):
        return pltpu.bitcast(jnp.broadcast_to(ref[s, r, :], (PEER_NKEYS // 2, TB)), BF16)

    def unit(ii, s):
        def run():
            i = jnp.minimum(i0 + ii, PEER_NKEYS - 1)
            gate = jnp.zeros((PEER_NKEYS, TB), BF16)
            for hd in range(PEER_HEADS):
                r = pl.ds(hd * PEER_NKEYS + i, 1)
                keys = slice(hd * PEER_NKEYS, (hd + 1) * PEER_NKEYS)
                e2 = e2_ref[s, keys, :]
                sel = jnp.where(rnk_ref[s, keys, :] < pair_row(cnt_ref, s, r), e2, jnp.zeros_like(e2))
                gate = gate + pair_row(e1_ref, s, r) * sel
            g_ref[ii * PEER_NKEYS:(ii + 1) * PEER_NKEYS, s * TB:(s + 1) * TB] = gate
        return run
    return [unit(ii, s) for ii in range(g_ref.shape[0] // PEER_NKEYS) for s in range(cnt_ref.shape[0])]


def _peer_dense_kernel(x1Tb_ref, u_ref, vT_ref, cnt_ref, rnk_ref, e1_ref, e2_ref, yT_ref,
                       ga_ref, gb_ref, h_ref, a_ref):
    ei = pl.program_id(1)
    te, tm = ga_ref.shape
    n_i = te // PEER_NKEYS
    nslab = cnt_ref.shape[0]
    gate_args = (cnt_ref, rnk_ref, e1_ref, e2_ref)

    @pl.when(ei == 0)
    def _():
        yT_ref[...] = jnp.zeros_like(yT_ref)
        for unit in _peer_gate_units(0, *gate_args, ga_ref):
            unit()

    D = u_ref.shape[1]
    kc = D // DENSE_H_CHUNKS
    rc = D // DENSE_Y_CHUNKS

    def h_unit(t, k):
        def run():
            xk = jnp.concatenate([x1Tb_ref[s, k * kc:(k + 1) * kc, :] for s in range(nslab)], axis=1)
            part = jnp.dot(u_ref[t * te:(t + 1) * te, k * kc:(k + 1) * kc], xk, preferred_element_type=F32)
            if k == 0:
                h_ref[t] = part
            else:
                h_ref[t] += part
        return run

    def act_unit(t, g_ref):
        def run():
            hT = h_ref[t]
            act = 0.5 * hT * (1.0 + lax.erf(hT * np.float32(1.0 / math.sqrt(2.0))))
            a_ref[t] = (act * g_ref[...].astype(F32)).astype(BF16)
        return run

    def y_unit(t, r):
        def run():
            y = jnp.dot(vT_ref[r * rc:(r + 1) * rc, t * te:(t + 1) * te], a_ref[t], preferred_element_type=F32)
            for s in range(nslab):
                yT_ref[s, r * rc:(r + 1) * rc, :] += y[:, s * TB:(s + 1) * TB]
        return run

    gb = _peer_gate_units((2 * ei + 1) * n_i, *gate_args, gb_ref)
    ga = _peer_gate_units((2 * ei + 2) * n_i, *gate_args, ga_ref)
    hA, hB = ([h_unit(t, k) for k in range(DENSE_H_CHUNKS)] for t in range(2))
    yA, yB = ([y_unit(t, r) for r in range(DENSE_Y_CHUNKS)] for t in range(2))
    actA, actB = act_unit(0, ga_ref), act_unit(1, gb_ref)
    mxu = hA + hB + yA + yB
    valu = gb[:len(gb) // 2] + [actA] + gb[len(gb) // 2:] + [actB] + ga
    need_before = {id(actA): hA[-1], id(actB): hB[-1], id(yA[0]): actA, id(yB[0]): actB}
    done, vi = set(), 0
    for n, m in enumerate(mxu):
        while id(m) in need_before and id(need_before[id(m)]) not in done:
            valu[vi]()
            done.add(id(valu[vi]))
            vi += 1
        m()
        done.add(id(m))
        target = -(-(n + 1) * len(valu) // len(mxu))
        while vi < target and not (id(valu[vi]) in need_before and id(need_before[id(valu[vi])]) not in done):
            valu[vi]()
            done.add(id(valu[vi]))
            vi += 1
    for v in valu[vi:]:
        v()


def _peer_dense(x1Tb, u, vT, gates, tm, te):
    nblk, D, _ = x1Tb.shape
    E = u.shape[0]
    W = PEER_HEADS * PEER_NKEYS
    nslab = tm // TB
    fm = lambda i, e: (i, 0, 0)
    return pl.pallas_call(
        _peer_dense_kernel,
        grid=(nblk // nslab, E // (2 * te)),
        in_specs=[
            pl.BlockSpec((nslab, D, TB), fm),
            pl.BlockSpec((2 * te, D), lambda i, e: (e, 0)),
            pl.BlockSpec((D, 2 * te), lambda i, e: (0, e)),
            pl.BlockSpec((nslab, W, TB), fm),
            pl.BlockSpec((nslab, W, TB), fm),
            pl.BlockSpec((nslab, W, TB), fm),
            pl.BlockSpec((nslab, W, TB), fm),
        ],
        out_specs=pl.BlockSpec((nslab, D, TB), fm),
        out_shape=jax.ShapeDtypeStruct((nblk, D, TB), F32),
        scratch_shapes=[pltpu.VMEM((te, tm), BF16), pltpu.VMEM((te, tm), BF16),
                        pltpu.VMEM((2, te, tm), F32), pltpu.VMEM((2, te, tm), BF16)],
        compiler_params=_params(("parallel", "arbitrary")),
        name="peer_dense",
    )(x1Tb, u, vT, *gates)


def _ln2_kernel(alpha, x1T_ref, yT_ref, g_ref, b_ref, o_ref):
    for s in range(x1T_ref.shape[0]):
        z = _layer_norm_cols(alpha * x1T_ref[s] + yT_ref[s], g_ref[...], b_ref[...])
        o_ref[s * TB:(s + 1) * TB, :] = z.T


def _ln2(x1T, yT, g, b, alpha, tm):
    nblk, D, _ = x1T.shape
    nslab = tm // TB
    fm = lambda i: (i, 0, 0)
    return pl.pallas_call(
        functools.partial(_ln2_kernel, alpha),
        grid=(nblk // nslab,),
        in_specs=[
            pl.BlockSpec((nslab, D, TB), fm),
            pl.BlockSpec((nslab, D, TB), fm),
            pl.BlockSpec((D, 1), lambda i: (0, 0)),
            pl.BlockSpec((D, 1), lambda i: (0, 0)),
        ],
        out_specs=pl.BlockSpec((tm, D), lambda i: (i, 0)),
        out_shape=jax.ShapeDtypeStruct((nblk * TB, D), F32),
        compiler_params=_params(("parallel",)),
        name="ln2_out",
    )(x1T, yT, g, b)


def _tiles(T):
    tm = 2 * TB if T % (2 * TB) == 0 else TB
    return dict(proj_tm=tm, proj_tn=FM_ROWS // 3, outproj_tm=tm, dense_tm=tm, dense_te=4 * PEER_NKEYS)


def kernel(x, w_in, w_o, lambda_q1, lambda_k1, lambda_q2, lambda_k2, subln_g, ln1_g, ln1_b,
           peer_wq, peer_k1, peer_k2, peer_u, peer_v, ln2_g, ln2_b):
    B, S, D = x.shape
    T = B * S
    depth = w_in.shape[0]
    assert S % TB == 0 and w_in.shape[2] == sum(IN_SPLITS)
    alpha = float((2 * depth) ** 0.25)
    slopes_diff, slopes_dsa = (s * F32(LOG2E) for s in _alibi_slopes())
    tiles = _tiles(T)
    offs = np.cumsum((0,) + IN_SPLITS)
    col = lambda w, k: w[:, offs[k]:offs[k + 1]]

    xt = x.reshape(T, D)
    for l in range(depth):
        w = w_in[l]
        dq, dk, dv, sq, sk, sv, iq, ik, iw = (col(w, k) for k in range(9))
        wt_fm = jnp.concatenate(
            [dq * (LOG2E * DIFF_QK_HALF ** -0.5), dv, sq * (LOG2E * DSA_HEAD_DIM ** -0.5), iq, sv],
            axis=1).T.astype(BF16)
        w_tok = jnp.concatenate([dk, sk, ik, ik], axis=1).astype(BF16)
        wt_iw = (iw * ((IDX_HEAD_DIM ** -0.5) * (N_IDX_HEADS ** -0.5))).T.astype(BF16)
        fm, tok, iwT = _in_proj(xt, wt_fm, w_tok, wt_iw, tiles["proj_tm"], tiles["proj_tn"])

        lam_init = 0.8 - 0.6 * math.exp(-0.3 * l)
        lam_params = jnp.stack([lambda_q1[l], lambda_k1[l], lambda_q2[l], lambda_k2[l]]).astype(F32)
        g_col = (subln_g[l].astype(F32) * (1.0 - lam_init)).reshape(DIFF_HEAD_DIM, 1)
        diffT = _diff_attn(fm, tok, slopes_diff, lam_params, g_col, lam_init, B, S)
        dsaT = _dsa_attn(fm, tok, iwT, slopes_dsa, B, S)

        colv = lambda p: p.reshape(D, 1).astype(F32)
        x1T, x1Tb = _out_proj(diffT, dsaT, w_o[l].T.astype(BF16), xt, colv(ln1_g[l]), colv(ln1_b[l]), alpha,
                              tiles["outproj_tm"])

        gates = _peer_gate(x1Tb, peer_wq[l].T.astype(BF16), peer_k1[l].astype(BF16), peer_k2[l].astype(BF16),
                           tiles["outproj_tm"])
        yT = _peer_dense(x1Tb, peer_u[l].astype(BF16), peer_v[l].T.astype(BF16), gates,
                         tiles["dense_tm"], tiles["dense_te"])
        xt = _ln2(x1T, yT, colv(ln2_g[l]), colv(ln2_b[l]), alpha, tiles["outproj_tm"])
    return xt.reshape(B, S, D)
```

```python
import functools
import math

import jax
import jax.numpy as jnp
import numpy as np
from jax import lax
from jax.experimental import pallas as pl
from jax.experimental.pallas import tpu as pltpu

F32 = jnp.float32
BF16 = jnp.bfloat16

N_DIFF_HEADS = 8
DIFF_HEAD_DIM = 128
DIFF_QK_HALF = DIFF_HEAD_DIM // 2
N_DSA_HEADS = 8
DSA_HEAD_DIM = 128
N_IDX_HEADS = 16
IDX_HEAD_DIM = 64
DSA_TOPK_MAX = 256
PEER_HEADS = 8
PEER_NKEYS = 128
PEER_QDIM = 256
PEER_HALF = PEER_QDIM // 2
PEER_TOPK = 16
LN_EPS = 1e-5
RMS_EPS = 1e-5

DIFF_W = N_DIFF_HEADS * DIFF_HEAD_DIM
DSA_QW = N_DSA_HEADS * DSA_HEAD_DIM
IDX_QW = N_IDX_HEADS * IDX_HEAD_DIM
IN_SPLITS = (DIFF_W, DIFF_W, DIFF_W, DSA_QW, DSA_HEAD_DIM, DSA_HEAD_DIM, IDX_QW, IDX_HEAD_DIM, N_IDX_HEADS)

TB = 256
FM_ROWS = DIFF_W * 2 + DSA_QW + IDX_QW + DSA_HEAD_DIM
TOK_COLS = DIFF_W + DSA_HEAD_DIM + 2 * IDX_HEAD_DIM
VMEM_LIMIT = 56 * 1024 * 1024
DENSE_H_CHUNKS = 4
DENSE_Y_CHUNKS = 4

LOG2E = math.log2(math.e)
NEG_BIG = -1e30
KEY_NEG_INF = (0xFF800000 ^ 0x7FFFFFFF) - 2 ** 32
INT_MIN = -(2 ** 31)

NT_DIMS = (((1,), (1,)), ((), ()))


def _alibi_slopes():
    n = N_DIFF_HEADS + N_DSA_HEADS
    s = 2.0 ** (-8.0 * np.arange(1, n + 1) / n)
    return (jnp.asarray(s[0::2], dtype=F32), jnp.asarray(s[1::2], dtype=F32))


def _params(sem):
    return pltpu.CompilerParams(dimension_semantics=sem, vmem_limit_bytes=VMEM_LIMIT)


def _in_proj_kernel(x_ref, wt_ref, wtok_ref, wiw_ref, fm_ref, tok_ref, iw_ref, xb_ref):
    nslab = fm_ref.shape[0]

    @pl.when(pl.program_id(1) == 0)
    def _():
        xb = x_ref[...].astype(BF16)
        xb_ref[...] = xb
        tok_ref[...] = jnp.dot(xb, wtok_ref[...], preferred_element_type=F32).astype(BF16)
        iw = lax.dot_general(wiw_ref[...], xb, NT_DIMS, preferred_element_type=F32)
        for s in range(nslab):
            iw_ref[s] = iw[:, s * TB:(s + 1) * TB]

    r = lax.dot_general(wt_ref[...], xb_ref[...], NT_DIMS, preferred_element_type=F32)
    for s in range(nslab):
        fm_ref[s] = r[:, s * TB:(s + 1) * TB].astype(BF16)


def _in_proj(x2d, wt_fm, w_tok, wt_iw, tm, tn):
    T, D = x2d.shape
    nslab = tm // TB
    return pl.pallas_call(
        _in_proj_kernel,
        grid=(T // tm, FM_ROWS // tn),
        in_specs=[
            pl.BlockSpec((tm, D), lambda i, j: (i, 0)),
            pl.BlockSpec((tn, D), lambda i, j: (j, 0)),
            pl.BlockSpec((D, TOK_COLS), lambda i, j: (0, 0)),
            pl.BlockSpec((N_IDX_HEADS, D), lambda i, j: (0, 0)),
        ],
        out_specs=[
            pl.BlockSpec((nslab, tn, TB), lambda i, j: (i, j, 0)),
            pl.BlockSpec((tm, TOK_COLS), lambda i, j: (i, 0)),
            pl.BlockSpec((nslab, N_IDX_HEADS, TB), lambda i, j: (i, 0, 0)),
        ],
        out_shape=[
            jax.ShapeDtypeStruct((T // TB, FM_ROWS, TB), BF16),
            jax.ShapeDtypeStruct((T, TOK_COLS), BF16),
            jax.ShapeDtypeStruct((T // TB, N_IDX_HEADS, TB), F32),
        ],
        scratch_shapes=[pltpu.VMEM((tm, D), BF16)],
        compiler_params=_params(("parallel", "arbitrary")),
        name="in_proj",
    )(x2d, wt_fm, w_tok, wt_iw)


def _rel_pos():
    kk = lax.broadcasted_iota(jnp.int32, (TB, TB), 0)
    qq = lax.broadcasted_iota(jnp.int32, (TB, TB), 1)
    return (qq - kk).astype(F32)


def _softmax_reset(m_ref, l_ref, acc_ref):
    m_ref[...] = jnp.full(m_ref.shape, NEG_BIG, F32)
    l_ref[...] = jnp.zeros(l_ref.shape, F32)
    acc_ref[...] = jnp.zeros(acc_ref.shape, F32)


def _diff_attn_kernel(lam_init, slopes_ref, lamp_ref, qT_ref, k_ref, vT_ref, g_ref, o_ref,
                      qm_ref, bias_ref, m_ref, l_ref, acc_ref, s_ref, p_ref, alpha_ref):
    qi = pl.program_id(1)
    lp = lamp_ref[...]
    lam = (jnp.exp(jnp.sum(lp[0:1] * lp[1:2], axis=1, keepdims=True))
           - jnp.exp(jnp.sum(lp[2:3] * lp[3:4], axis=1, keepdims=True)) + lam_init)
    rel = _rel_pos()
    row = lax.broadcasted_iota(jnp.int32, (DIFF_HEAD_DIM, TB), 0)
    hd = lambda h: slice(h * DIFF_HEAD_DIM, (h + 1) * DIFF_HEAD_DIM)

    _softmax_reset(m_ref, l_ref, acc_ref)
    rel2 = jnp.concatenate([rel, rel], axis=1)
    for h in range(N_DIFF_HEADS):
        qT = qT_ref[0, hd(h), :]
        zero = jnp.zeros_like(qT)
        qm_ref[h] = jnp.concatenate(
            [jnp.where(row < DIFF_QK_HALF, qT, zero), jnp.where(row >= DIFF_QK_HALF, qT, zero)], axis=1)
        bias_ref[h] = slopes_ref[h] * rel2

    def step(kj, diag):
        rows = pl.ds(pl.multiple_of(kj * TB, TB), TB)
        dist0 = ((qi - kj) * TB).astype(F32)
        for h in range(N_DIFF_HEADS):
            s_ref[h] = jnp.dot(k_ref[rows, hd(h)], qm_ref[h], preferred_element_type=F32)
        for h in range(N_DIFF_HEADS):
            s = s_ref[h] - bias_ref[h]
            if diag:
                s = jnp.where(rel2 >= 0, s, NEG_BIG)
            shift = slopes_ref[h] * dist0
            m_old = m_ref[h]
            m_new = jnp.maximum(m_old, jnp.max(s, axis=0, keepdims=True) - shift)
            alpha = jnp.exp2(m_old - m_new)
            p = jnp.exp2(s - (m_new + shift))
            l_ref[h] = alpha * l_ref[h] + jnp.sum(p, axis=0, keepdims=True)
            p_ref[h] = p.astype(BF16)
            alpha_ref[h] = alpha
            m_ref[h] = m_new
        for h in range(N_DIFF_HEADS):
            acc_ref[h] = alpha_ref[h] * acc_ref[h] + jnp.dot(vT_ref[kj, hd(h), :], p_ref[h],
                                                               preferred_element_type=F32)

    def body(kj, c):
        step(kj, False)
        return c

    lax.fori_loop(0, qi, body, 0)
    step(qi, True)
    for h in range(N_DIFF_HEADS):
        w = acc_ref[h] / l_ref[h]
        out = w[:, :TB] - lam * w[:, TB:]
        ms = jnp.mean(out * out, axis=0, keepdims=True)
        o_ref[0, hd(h), :] = (out * lax.rsqrt(ms + RMS_EPS) * g_ref[...]).astype(BF16)


def _diff_attn(fm, tok, slopes, lam_params, g_col, lam_init, B, S):
    nq = S // TB
    T = B * S
    return pl.pallas_call(
        functools.partial(_diff_attn_kernel, lam_init),
        grid=(B, nq),
        in_specs=[
            pl.BlockSpec(memory_space=pltpu.SMEM),
            pl.BlockSpec((4, DIFF_QK_HALF), lambda b, q: (0, 0)),
            pl.BlockSpec((1, DIFF_W, TB), lambda b, q: (b * nq + q, 0, 0)),
            pl.BlockSpec((S, DIFF_W), lambda b, q: (b, 0)),
            pl.BlockSpec((nq, DIFF_W, TB), lambda b, q: (b, 1, 0)),
            pl.BlockSpec((DIFF_HEAD_DIM, 1), lambda b, q: (0, 0)),
        ],
        out_specs=pl.BlockSpec((1, DIFF_W, TB), lambda b, q: (b * nq + q, 0, 0)),
        out_shape=jax.ShapeDtypeStruct((T // TB, DIFF_W, TB), BF16),
        scratch_shapes=[
            pltpu.VMEM((N_DIFF_HEADS, DIFF_HEAD_DIM, 2 * TB), BF16),
            pltpu.VMEM((N_DIFF_HEADS, TB, 2 * TB), F32),
            pltpu.VMEM((N_DIFF_HEADS, 1, 2 * TB), F32),
            pltpu.VMEM((N_DIFF_HEADS, 1, 2 * TB), F32),
            pltpu.VMEM((N_DIFF_HEADS, DIFF_HEAD_DIM, 2 * TB), F32),
            pltpu.VMEM((N_DIFF_HEADS, TB, 2 * TB), F32),
            pltpu.VMEM((N_DIFF_HEADS, TB, 2 * TB), BF16),
            pltpu.VMEM((N_DIFF_HEADS, 1, 2 * TB), F32),
        ],
        compiler_params=_params(("parallel", "arbitrary")),
        name="diff_attn",
    )(slopes, lam_params, fm, tok, fm, g_col)


def _dsa_attn_kernel(topk, slopes_ref, sqT_ref, iqT_ref, svT_ref, sk_ref, ikk_ref, iwT_ref, o_ref, keys_ref,
                     iqall_ref, bias_ref, qall_ref, m_ref, l_ref, acc_ref, p_ref, alpha_ref):
    qi = pl.program_id(1)
    rel = _rel_pos()
    iw = iwT_ref[0]
    row = lax.broadcasted_iota(jnp.int32, (2 * IDX_HEAD_DIM, TB), 0)

    for p in range(N_IDX_HEADS // 2):
        pair = iqT_ref[0, p * 2 * IDX_HEAD_DIM:(p + 1) * 2 * IDX_HEAD_DIM, :]
        zero = jnp.zeros_like(pair)
        iqall_ref[:, (2 * p) * TB:(2 * p + 1) * TB] = jnp.where(row < IDX_HEAD_DIM, pair, zero)
        iqall_ref[:, (2 * p + 1) * TB:(2 * p + 2) * TB] = jnp.where(row >= IDX_HEAD_DIM, pair, zero)

    def score_block(kj, diag):
        rows = pl.ds(pl.multiple_of(kj * TB, TB), TB)
        ikk = ikk_ref[rows, :]
        acc = jnp.zeros((TB, TB), F32)
        half = N_IDX_HEADS // 2
        for g in range(2):
            r = jnp.dot(ikk, iqall_ref[:, g * half * TB:(g + 1) * half * TB], preferred_element_type=F32)
            for j in range(half):
                hh = g * half + j
                acc = acc + iw[hh:hh + 1, :] * jnp.maximum(r[:, j * TB:(j + 1) * TB], 0.0)
        if diag:
            acc = jnp.where(rel >= 0, acc, -jnp.inf)
        bits = pltpu.bitcast(acc, jnp.int32)
        keys_ref[rows, :] = jnp.where(bits >= 0, bits, bits ^ jnp.int32(0x7FFFFFFF))

    def score_body(kj, c):
        score_block(kj, False)
        return c

    lax.fori_loop(0, qi, score_body, 0)
    score_block(qi, True)

    def count_ge(cand):
        def body(kj, cnt):
            blk = keys_ref[pl.ds(pl.multiple_of(kj * TB, TB), TB), :]
            hit = jnp.where(blk >= cand, 1, 0).astype(jnp.int32)
            for r in range(TB // 8):
                cnt = cnt + hit[r * 8:(r + 1) * 8]
            return cnt
        cnt = lax.fori_loop(0, qi + 1, body, jnp.zeros((8, TB), jnp.int32))
        return jnp.sum(cnt, axis=0, keepdims=True)

    lo = jnp.where(count_ge(jnp.zeros((1, TB), jnp.int32)) >= topk, jnp.int32(0), jnp.int32(INT_MIN))

    def bit_body(t, lo):
        cand = lo + lax.shift_left(jnp.int32(1), jnp.int32(30) - t)
        return jnp.where(count_ge(cand) >= topk, cand, lo)

    lo = lax.fori_loop(0, 31, bit_body, lo)
    thr = jnp.maximum(lo, jnp.int32(KEY_NEG_INF + 1))

    hd = lambda h: slice(h * DSA_HEAD_DIM, (h + 1) * DSA_HEAD_DIM)
    lanes = lambda h: slice(h * TB, (h + 1) * TB)
    _softmax_reset(m_ref, l_ref, acc_ref)
    for h in range(N_DSA_HEADS):
        bias_ref[:, lanes(h)] = slopes_ref[h] * rel
        qall_ref[:, lanes(h)] = sqT_ref[0, hd(h), :]
    slope_row = jnp.concatenate([jnp.full((1, TB), slopes_ref[h], F32) for h in range(N_DSA_HEADS)], axis=1)

    def attend(kj, c):
        rows = pl.ds(pl.multiple_of(kj * TB, TB), TB)
        dist0 = ((qi - kj) * TB).astype(F32)
        s_all = jnp.dot(sk_ref[rows, :], qall_ref[...], preferred_element_type=F32)
        sel = keys_ref[rows, :] >= thr
        for ct in range(N_DSA_HEADS * TB // 128):
            cl = slice(ct * 128, (ct + 1) * 128)
            ql = slice((ct * 128) % TB, (ct * 128) % TB + 128)
            s = jnp.where(sel[:, ql], s_all[:, cl] - bias_ref[:, cl], NEG_BIG)
            shift = slope_row[:, cl] * dist0
            m_old = m_ref[0, :, cl]
            m_new = jnp.maximum(m_old, jnp.max(s, axis=0, keepdims=True) - shift)
            m_ref[0, :, cl] = m_new
            alpha_ref[:, cl] = jnp.exp2(m_old - m_new)
            p = jnp.exp2(s - (m_new + shift))
            l_ref[0, :, cl] = alpha_ref[:, cl] * l_ref[0, :, cl] + jnp.sum(p, axis=0, keepdims=True)
            p_ref[:, cl] = p.astype(BF16)
        acc_ref[0] = alpha_ref[...] * acc_ref[0] + jnp.dot(svT_ref[kj], p_ref[...], preferred_element_type=F32)
        return c

    lax.fori_loop(0, qi + 1, attend, 0)
    out = acc_ref[0] / l_ref[0]
    for h in range(N_DSA_HEADS):
        o_ref[0, hd(h), :] = out[:, lanes(h)].astype(BF16)


def _dsa_attn(fm, tok, iwT, slopes, B, S):
    nq = S // TB
    T = B * S
    topk = min(DSA_TOPK_MAX, S // 4)
    sq_blk = (2 * DIFF_W) // DSA_QW
    iq_blk = (2 * DIFF_W + DSA_QW) // IDX_QW
    sv_blk = (2 * DIFF_W + DSA_QW + IDX_QW) // DSA_HEAD_DIM
    sk_blk = DIFF_W // DSA_HEAD_DIM
    return pl.pallas_call(
        functools.partial(_dsa_attn_kernel, topk),
        grid=(B, nq),
        in_specs=[
            pl.BlockSpec(memory_space=pltpu.SMEM),
            pl.BlockSpec((1, DSA_QW, TB), lambda b, q: (b * nq + q, sq_blk, 0)),
            pl.BlockSpec((1, IDX_QW, TB), lambda b, q: (b * nq + q, iq_blk, 0)),
            pl.BlockSpec((nq, DSA_HEAD_DIM, TB), lambda b, q: (b, sv_blk, 0)),
            pl.BlockSpec((S, DSA_HEAD_DIM), lambda b, q: (b, sk_blk)),
            pl.BlockSpec((S, 2 * IDX_HEAD_DIM), lambda b, q: (b, sk_blk + 1)),
            pl.BlockSpec((1, N_IDX_HEADS, TB), lambda b, q: (b * nq + q, 0, 0)),
        ],
        out_specs=pl.BlockSpec((1, DSA_QW, TB), lambda b, q: (b * nq + q, 0, 0)),
        out_shape=jax.ShapeDtypeStruct((T // TB, DSA_QW, TB), BF16),
        scratch_shapes=[
            pltpu.VMEM((S, TB), jnp.int32),
            pltpu.VMEM((2 * IDX_HEAD_DIM, N_IDX_HEADS * TB), BF16),
            pltpu.VMEM((TB, N_DSA_HEADS * TB), F32),
            pltpu.VMEM((DSA_HEAD_DIM, N_DSA_HEADS * TB), BF16),
            pltpu.VMEM((1, 1, N_DSA_HEADS * TB), F32),
            pltpu.VMEM((1, 1, N_DSA_HEADS * TB), F32),
            pltpu.VMEM((1, DSA_HEAD_DIM, N_DSA_HEADS * TB), F32),
            pltpu.VMEM((TB, N_DSA_HEADS * TB), BF16),
            pltpu.VMEM((1, N_DSA_HEADS * TB), F32),
        ],
        compiler_params=_params(("parallel", "arbitrary")),
        name="dsa_attn",
    )(slopes, fm, fm, fm, tok, tok, iwT)


def _layer_norm_cols(y, g, b):
    mu = jnp.mean(y, axis=0, keepdims=True)
    d = y - mu
    var = jnp.mean(d * d, axis=0, keepdims=True)
    return d * lax.rsqrt(var + LN_EPS) * g + b


def _out_proj_kernel(alpha, diffT_ref, dsaT_ref, woT_ref, x_ref, g_ref, b_ref, x1T_ref, x1Tb_ref):
    nslab = diffT_ref.shape[0]
    attnT = jnp.concatenate(
        [jnp.concatenate([diffT_ref[s], dsaT_ref[s]], axis=0) for s in range(nslab)], axis=1)
    mixedT = jnp.dot(woT_ref[...], attnT, preferred_element_type=F32)
    for s in range(nslab):
        xT = x_ref[s * TB:(s + 1) * TB, :].T
        y = _layer_norm_cols(alpha * xT + mixedT[:, s * TB:(s + 1) * TB], g_ref[...], b_ref[...])
        x1T_ref[s] = y
        x1Tb_ref[s] = y.astype(BF16)


def _out_proj(diffT, dsaT, woT, x2d, g, b, alpha, tm):
    T, D = x2d.shape
    nslab = tm // TB
    fm = lambda i: (i, 0, 0)
    return pl.pallas_call(
        functools.partial(_out_proj_kernel, alpha),
        grid=(T // tm,),
        in_specs=[
            pl.BlockSpec((nslab, DIFF_W, TB), fm),
            pl.BlockSpec((nslab, DSA_QW, TB), fm),
            pl.BlockSpec((D, DIFF_W + DSA_QW), lambda i: (0, 0)),
            pl.BlockSpec((tm, D), lambda i: (i, 0)),
            pl.BlockSpec((D, 1), lambda i: (0, 0)),
            pl.BlockSpec((D, 1), lambda i: (0, 0)),
        ],
        out_specs=[pl.BlockSpec((nslab, D, TB), fm), pl.BlockSpec((nslab, D, TB), fm)],
        out_shape=[jax.ShapeDtypeStruct((T // TB, D, TB), F32), jax.ShapeDtypeStruct((T // TB, D, TB), BF16)],
        compiler_params=_params(("parallel",)),
        name="out_proj_ln1",
    )(diffT, dsaT, woT, x2d, g, b)


def _top_values(s, k):
    vals = []
    for _ in range(k):
        m = jnp.max(s, axis=0, keepdims=True)
        vals.append(m)
        s = jnp.where(s == m, -jnp.inf, s)
    return vals


def _top_values_ranked(s, k):
    vals = []
    rank = jnp.full(s.shape, float(k), F32)
    for b in range(k):
        m = jnp.max(s, axis=0, keepdims=True)
        vals.append(m)
        hit = s == m
        s = jnp.where(hit, -jnp.inf, s)
        rank = jnp.where(hit, float(b), rank)
    return vals, rank


def _bf16_pair_words(x):
    bits = pltpu.bitcast(x, jnp.uint32)
    bits = bits + jnp.uint32(0x7FFF) + (lax.shift_right_logical(bits, jnp.uint32(16)) & jnp.uint32(1))
    hi = bits & jnp.uint32(0xFFFF0000)
    return hi | lax.shift_right_logical(hi, jnp.uint32(16))


def _peer_gate_kernel(x1Tb_ref, wqT_ref, k1_ref, k2_ref, cnt_ref, rnk_ref, e1_ref, e2_ref):
    nslab = x1Tb_ref.shape[0]
    x1cat = jnp.concatenate([x1Tb_ref[s] for s in range(nslab)], axis=1)
    qT = jnp.dot(wqT_ref[...], x1cat, preferred_element_type=F32).astype(BF16)
    for h in range(PEER_HEADS):
        base = h * PEER_QDIM
        s1w = jnp.dot(k1_ref[...], qT[base:base + PEER_HALF], preferred_element_type=F32)
        s2w = jnp.dot(k2_ref[...], qT[base + PEER_HALF:base + PEER_QDIM], preferred_element_type=F32)
        for s in range(nslab):
            s1 = s1w[:, s * TB:(s + 1) * TB]
            s2 = s2w[:, s * TB:(s + 1) * TB]
            v1 = _top_values(s1, PEER_TOPK)
            v2, rank2 = _top_values_ranked(s2, PEER_TOPK)
            v1c = jnp.concatenate(v1, axis=0)
            v2c = jnp.concatenate(v2, axis=0)
            half = PEER_TOPK // 2
            cand = jnp.concatenate(
                [v1[0] + v2c] + [v1[a] + v2c[:half] for a in range(1, half)] + [v1c[half:] + v2[0]], axis=0)
            tau = _top_values(cand, PEER_TOPK)[-1]
            m1, m2 = v1[0], v2[0]
            z = jnp.sum(jnp.where(cand >= tau, jnp.exp(cand - (m1 + m2)), 0.0), axis=0, keepdims=True)
            cnt = jnp.zeros(s1.shape, F32)
            for vb in v2:
                cnt = cnt + jnp.where(s1 + vb >= tau, 1.0, 0.0)
            rows = slice(h * PEER_NKEYS, (h + 1) * PEER_NKEYS)
            cnt_ref[s, rows, :] = _bf16_pair_words(cnt)
            rnk_ref[s, rows, :] = rank2.astype(BF16)
            e1_ref[s, rows, :] = _bf16_pair_words(jnp.exp(s1 - m1) / z)
            e2_ref[s, rows, :] = jnp.exp(s2 - m2).astype(BF16)


def _peer_gate(x1Tb, wqT, k1, k2, tm):
    nblk, D, _ = x1Tb.shape
    W = PEER_HEADS * PEER_NKEYS
    nslab = tm // TB
    fm = lambda i: (i, 0, 0)
    return pl.pallas_call(
        _peer_gate_kernel,
        grid=(nblk // nslab,),
        in_specs=[
            pl.BlockSpec((nslab, D, TB), fm),
            pl.BlockSpec((PEER_HEADS * PEER_QDIM, D), lambda i: (0, 0)),
            pl.BlockSpec((PEER_NKEYS, PEER_HALF), lambda i: (0, 0)),
            pl.BlockSpec((PEER_NKEYS, PEER_HALF), lambda i: (0, 0)),
        ],
        out_specs=[pl.BlockSpec((nslab, W, TB), fm)] * 4,
        out_shape=[jax.ShapeDtypeStruct((nblk, W, TB), dt) for dt in (jnp.uint32, BF16, jnp.uint32, BF16)],
        compiler_params=_params(("parallel",)),
        name="peer_gate",
    )(x1Tb, wqT, k1, k2)


def _peer_gate_units(i0, cnt_ref, rnk_ref, e1_ref, e2_ref, g_ref):
    def pair_row(ref, s, r):
        return pltpu.bitcast(jnp.broadcast_to(ref[s, r, :], (PEER_NKEYS // 2, TB)), BF16)

    def unit(ii, s):
        def run():
            i = jnp.minimum(i0 + ii, PEER_NKEYS - 1)
            gate = jnp.zeros((PEER_NKEYS, TB), BF16)
            for hd in range(PEER_HEADS):
                r = pl.ds(hd * PEER_NKEYS + i, 1)
                keys = slice(hd * PEER_NKEYS, (hd + 1) * PEER_NKEYS)
                e2 = e2_ref[s, keys, :]
                sel = jnp.where(rnk_ref[s, keys, :] < pair_row(cnt_ref, s, r), e2, jnp.zeros_like(e2))
                gate = gate + pair_row(e1_ref, s, r) * sel
            g_ref[ii * PEER_NKEYS:(ii + 1) * PEER_NKEYS, s * TB:(s + 1) * TB] = gate
        return run
    return [unit(ii, s) for ii in range(g_ref.shape[0] // PEER_NKEYS) for s in range(cnt_ref.shape[0])]


def _peer_dense_kernel(x1Tb_ref, u_ref, vT_ref, cnt_ref, rnk_ref, e1_ref, e2_ref, yT_ref,
                       ga_ref, gb_ref, h_ref, a_ref):
    ei = pl.program_id(1)
    te, tm = ga_ref.shape
    n_i = te // PEER_NKEYS
    nslab = cnt_ref.shape[0]
    gate_args = (cnt_ref, rnk_ref, e1_ref, e2_ref)

    @pl.when(ei == 0)
    def _():
        yT_ref[...] = jnp.zeros_like(yT_ref)
        for unit in _peer_gate_units(0, *gate_args, ga_ref):
            unit()

    D = u_ref.shape[1]
    kc = D // DENSE_H_CHUNKS
    rc = D // DENSE_Y_CHUNKS

    def h_unit(t, k):
        def run():
            xk = jnp.concatenate([x1Tb_ref[s, k * kc:(k + 1) * kc, :] for s in range(nslab)], axis=1)
            part = jnp.dot(u_ref[t * te:(t + 1) * te, k * kc:(k + 1) * kc], xk, preferred_element_type=F32)
            if k == 0:
                h_ref[t] = part
            else:
                h_ref[t] += part
        return run

    def act_unit(t, g_ref):
        def run():
            hT = h_ref[t]
            act = 0.5 * hT * (1.0 + lax.erf(hT * np.float32(1.0 / math.sqrt(2.0))))
            a_ref[t] = (act * g_ref[...].astype(F32)).astype(BF16)
        return run

    def y_unit(t, r):
        def run():
            y = jnp.dot(vT_ref[r * rc:(r + 1) * rc, t * te:(t + 1) * te], a_ref[t], preferred_element_type=F32)
            for s in range(nslab):
                yT_ref[s, r * rc:(r + 1) * rc, :] += y[:, s * TB:(s + 1) * TB]
        return run

    gb = _peer_gate_units((2 * ei + 1) * n_i, *gate_args, gb_ref)
    ga = _peer_gate_units((2 * ei + 2) * n_i, *gate_args, ga_ref)
    hA, hB = ([h_unit(t, k) for k in range(DENSE_H_CHUNKS)] for t in range(2))
    yA, yB = ([y_unit(t, r) for r in range(DENSE_Y_CHUNKS)] for t in range(2))
    actA, actB = act_unit(0, ga_ref), act_unit(1, gb_ref)
    mxu = hA + hB + yA + yB
    valu = gb[:len(gb) // 2] + [actA] + gb[len(gb) // 2:] + [actB] + ga
    need_before = {id(actA): hA[-1], id(actB): hB[-1], id(yA[0]): actA, id(yB[0]): actB}
    done, vi = set(), 0
    for n, m in enumerate(mxu):
        while id(m) in need_before and id(need_before[id(m)]) not in done:
            valu[vi]()
            done.add(id(valu[vi]))
            vi += 1
        m()
        done.add(id(m))
        target = -(-(n + 1) * len(valu) // len(mxu))
        while vi < target and not (id(valu[vi]) in need_before and id(need_before[id(valu[vi])]) not in done):
            valu[vi]()
            done.add(id(valu[vi]))
            vi += 1
    for v in valu[vi:]:
        v()


def _peer_dense(x1Tb, u, vT, gates, tm, te):
    nblk, D, _ = x1Tb.shape
    E = u.shape[0]
    W = PEER_HEADS * PEER_NKEYS
    nslab = tm // TB
    fm = lambda i, e: (i, 0, 0)
    return pl.pallas_call(
        _peer_dense_kernel,
        grid=(nblk // nslab, E // (2 * te)),
        in_specs=[
            pl.BlockSpec((nslab, D, TB), fm),
            pl.BlockSpec((2 * te, D), lambda i, e: (e, 0)),
            pl.BlockSpec((D, 2 * te), lambda i, e: (0, e)),
            pl.BlockSpec((nslab, W, TB), fm),
            pl.BlockSpec((nslab, W, TB), fm),
            pl.BlockSpec((nslab, W, TB), fm),
            pl.BlockSpec((nslab, W, TB), fm),
        ],
        out_specs=pl.BlockSpec((nslab, D, TB), fm),
        out_shape=jax.ShapeDtypeStruct((nblk, D, TB), F32),
        scratch_shapes=[pltpu.VMEM((te, tm), BF16), pltpu.VMEM((te, tm), BF16),
                        pltpu.VMEM((2, te, tm), F32), pltpu.VMEM((2, te, tm), BF16)],
        compiler_params=_params(("parallel", "arbitrary")),
        name="peer_dense",
    )(x1Tb, u, vT, *gates)


def _ln2_kernel(alpha, x1T_ref, yT_ref, g_ref, b_ref, o_ref):
    for s in range(x1T_ref.shape[0]):
        z = _layer_norm_cols(alpha * x1T_ref[s] + yT_ref[s], g_ref[...], b_ref[...])
        o_ref[s * TB:(s + 1) * TB, :] = z.T


def _ln2(x1T, yT, g, b, alpha, tm):
    nblk, D, _ = x1T.shape
    nslab = tm // TB
    fm = lambda i: (i, 0, 0)
    return pl.pallas_call(
        functools.partial(_ln2_kernel, alpha),
        grid=(nblk // nslab,),
        in_specs=[
            pl.BlockSpec((nslab, D, TB), fm),
            pl.BlockSpec((nslab, D, TB), fm),
            pl.BlockSpec((D, 1), lambda i: (0, 0)),
            pl.BlockSpec((D, 1), lambda i: (0, 0)),
        ],
        out_specs=pl.BlockSpec((tm, D), lambda i: (i, 0)),
        out_shape=jax.ShapeDtypeStruct((nblk * TB, D), F32),
        compiler_params=_params(("parallel",)),
        name="ln2_out",
    )(x1T, yT, g, b)


def _tiles(T):
    tm = 2 * TB if T % (2 * TB) == 0 else TB
    return dict(proj_tm=tm, proj_tn=FM_ROWS // 3, outproj_tm=tm, dense_tm=tm, dense_te=4 * PEER_NKEYS)


def kernel(x, w_in, w_o, lambda_q1, lambda_k1, lambda_q2, lambda_k2, subln_g, ln1_g, ln1_b,
           peer_wq, peer_k1, peer_k2, peer_u, peer_v, ln2_g, ln2_b):
    B, S, D = x.shape
    T = B * S
    depth = w_in.shape[0]
    assert S % TB == 0 and w_in.shape[2] == sum(IN_SPLITS)
    alpha = float((2 * depth) ** 0.25)
    slopes_diff, slopes_dsa = (s * F32(LOG2E) for s in _alibi_slopes())
    tiles = _tiles(T)
    offs = np.cumsum((0,) + IN_SPLITS)
    col = lambda w, k: w[:, offs[k]:offs[k + 1]]

    xt = x.reshape(T, D)
    for l in range(depth):
        w = w_in[l]
        dq, dk, dv, sq, sk, sv, iq, ik, iw = (col(w, k) for k in range(9))
        wt_fm = jnp.concatenate(
            [dq * (LOG2E * DIFF_QK_HALF ** -0.5), dv, sq * (LOG2E * DSA_HEAD_DIM ** -0.5), iq, sv],
            axis=1).T.astype(BF16)
        w_tok = jnp.concatenate([dk, sk, ik, ik], axis=1).astype(BF16)
        wt_iw = (iw * ((IDX_HEAD_DIM ** -0.5) * (N_IDX_HEADS ** -0.5))).T.astype(BF16)
        fm, tok, iwT = _in_proj(xt, wt_fm, w_tok, wt_iw, tiles["proj_tm"], tiles["proj_tn"])

        lam_init = 0.8 - 0.6 * math.exp(-0.3 * l)
        lam_params = jnp.stack([lambda_q1[l], lambda_k1[l], lambda_q2[l], lambda_k2[l]]).astype(F32)
        g_col = (subln_g[l].astype(F32) * (1.0 - lam_init)).reshape(DIFF_HEAD_DIM, 1)
        diffT = _diff_attn(fm, tok, slopes_diff, lam_params, g_col, lam_init, B, S)
        dsaT = _dsa_attn(fm, tok, iwT, slopes_dsa, B, S)

        colv = lambda p: p.reshape(D, 1).astype(F32)
        x1T, x1Tb = _out_proj(diffT, dsaT, w_o[l].T.astype(BF16), xt, colv(ln1_g[l]), colv(ln1_b[l]), alpha,
                              tiles["outproj_tm"])

        gates = _peer_gate(x1Tb, peer_wq[l].T.astype(BF16), peer_k1[l].astype(BF16), peer_k2[l].astype(BF16),
                           tiles["outproj_tm"])
        yT = _peer_dense(x1Tb, peer_u[l].astype(BF16), peer_v[l].T.astype(BF16), gates,
                         tiles["dense_tm"], tiles["dense_te"])
        xt = _ln2(x1T, yT, colv(ln2_g[l]), colv(ln2_b[l]), alpha, tiles["outproj_tm"])
    return xt.reshape(B, S, D)
```

```python
import functools
import math

import jax
import jax.numpy as jnp
import numpy as np
from jax import lax
from jax.experimental import pallas as pl
from jax.experimental.pallas import tpu as pltpu

F32 = jnp.float32
BF16 = jnp.bfloat16

N_DIFF_HEADS = 8
DIFF_HEAD_DIM = 128
DIFF_QK_HALF = DIFF_HEAD_DIM // 2
N_DSA_HEADS = 8
DSA_HEAD_DIM = 128
N_IDX_HEADS = 16
IDX_HEAD_DIM = 64
DSA_TOPK_MAX = 256
PEER_HEADS = 8
PEER_NKEYS = 128
PEER_QDIM = 256
PEER_HALF = PEER_QDIM // 2
PEER_TOPK = 16
LN_EPS = 1e-5
RMS_EPS = 1e-5

DIFF_W = N_DIFF_HEADS * DIFF_HEAD_DIM
DSA_QW = N_DSA_HEADS * DSA_HEAD_DIM
IDX_QW = N_IDX_HEADS * IDX_HEAD_DIM
IN_SPLITS = (DIFF_W, DIFF_W, DIFF_W, DSA_QW, DSA_HEAD_DIM, DSA_HEAD_DIM, IDX_QW, IDX_HEAD_DIM, N_IDX_HEADS)

TB = 256
FM_ROWS = DIFF_W * 2 + DSA_QW + IDX_QW + DSA_HEAD_DIM
TOK_COLS = DIFF_W + DSA_HEAD_DIM + 2 * IDX_HEAD_DIM
VMEM_LIMIT = 56 * 1024 * 1024
DENSE_H_CHUNKS = 4
DENSE_Y_CHUNKS = 4

LOG2E = math.log2(math.e)
NEG_BIG = -1e30
KEY_NEG_INF = (0xFF800000 ^ 0x7FFFFFFF) - 2 ** 32
INT_MIN = -(2 ** 31)

NT_DIMS = (((1,), (1,)), ((), ()))


def _alibi_slopes():
    n = N_DIFF_HEADS + N_DSA_HEADS
    s = 2.0 ** (-8.0 * np.arange(1, n + 1) / n)
    return (jnp.asarray(s[0::2], dtype=F32), jnp.asarray(s[1::2], dtype=F32))


def _params(sem):
    return pltpu.CompilerParams(dimension_semantics=sem, vmem_limit_bytes=VMEM_LIMIT)


def _in_proj_kernel(x_ref, wt_ref, wtok_ref, wiw_ref, fm_ref, tok_ref, iw_ref, xb_ref):
    nslab = fm_ref.shape[0]

    @pl.when(pl.program_id(1) == 0)
    def _():
        xb = x_ref[...].astype(BF16)
        xb_ref[...] = xb
        tok_ref[...] = jnp.dot(xb, wtok_ref[...], preferred_element_type=F32).astype(BF16)
        iw = lax.dot_general(wiw_ref[...], xb, NT_DIMS, preferred_element_type=F32)
        for s in range(nslab):
            iw_ref[s] = iw[:, s * TB:(s + 1) * TB]

    r = lax.dot_general(wt_ref[...], xb_ref[...], NT_DIMS, preferred_element_type=F32)
    for s in range(nslab):
        fm_ref[s] = r[:, s * TB:(s + 1) * TB].astype(BF16)


def _in_proj(x2d, wt_fm, w_tok, wt_iw, tm, tn):
    T, D = x2d.shape
    nslab = tm // TB
    return pl.pallas_call(
        _in_proj_kernel,
        grid=(T // tm, FM_ROWS // tn),
        in_specs=[
            pl.BlockSpec((tm, D), lambda i, j: (i, 0)),
            pl.BlockSpec((tn, D), lambda i, j: (j, 0)),
            pl.BlockSpec((D, TOK_COLS), lambda i, j: (0, 0)),
            pl.BlockSpec((N_IDX_HEADS, D), lambda i, j: (0, 0)),
        ],
        out_specs=[
            pl.BlockSpec((nslab, tn, TB), lambda i, j: (i, j, 0)),
            pl.BlockSpec((tm, TOK_COLS), lambda i, j: (i, 0)),
            pl.BlockSpec((nslab, N_IDX_HEADS, TB), lambda i, j: (i, 0, 0)),
        ],
        out_shape=[
            jax.ShapeDtypeStruct((T // TB, FM_ROWS, TB), BF16),
            jax.ShapeDtypeStruct((T, TOK_COLS), BF16),
            jax.ShapeDtypeStruct((T // TB, N_IDX_HEADS, TB), F32),
        ],
        scratch_shapes=[pltpu.VMEM((tm, D), BF16)],
        compiler_params=_params(("parallel", "arbitrary")),
        name="in_proj",
    )(x2d, wt_fm, w_tok, wt_iw)


def _rel_pos():
    kk = lax.broadcasted_iota(jnp.int32, (TB, TB), 0)
    qq = lax.broadcasted_iota(jnp.int32, (TB, TB), 1)
    return (qq - kk).astype(F32)


def _softmax_reset(m_ref, l_ref, acc_ref):
    m_ref[...] = jnp.full(m_ref.shape, NEG_BIG, F32)
    l_ref[...] = jnp.zeros(l_ref.shape, F32)
    acc_ref[...] = jnp.zeros(acc_ref.shape, F32)


def _diff_attn_kernel(lam_init, slopes_ref, lamp_ref, qT_ref, k_ref, vT_ref, g_ref, o_ref,
                      qm_ref, bias_ref, m_ref, l_ref, acc_ref, s_ref, p_ref, alpha_ref):
    qi = pl.program_id(1)
    lp = lamp_ref[...]
    lam = (jnp.exp(jnp.sum(lp[0:1] * lp[1:2], axis=1, keepdims=True))
           - jnp.exp(jnp.sum(lp[2:3] * lp[3:4], axis=1, keepdims=True)) + lam_init)
    rel = _rel_pos()
    row = lax.broadcasted_iota(jnp.int32, (DIFF_HEAD_DIM, TB), 0)
    hd = lambda h: slice(h * DIFF_HEAD_DIM, (h + 1) * DIFF_HEAD_DIM)

    _softmax_reset(m_ref, l_ref, acc_ref)
    rel2 = jnp.concatenate([rel, rel], axis=1)
    for h in range(N_DIFF_HEADS):
        qT = qT_ref[0, hd(h), :]
        zero = jnp.zeros_like(qT)
        qm_ref[h] = jnp.concatenate(
            [jnp.where(row < DIFF_QK_HALF, qT, zero), jnp.where(row >= DIFF_QK_HALF, qT, zero)], axis=1)
        bias_ref[h] = slopes_ref[h] * rel2

    def step(kj, diag):
        rows = pl.ds(pl.multiple_of(kj * TB, TB), TB)
        dist0 = ((qi - kj) * TB).astype(F32)
        for h in range(N_DIFF_HEADS):
            s_ref[h] = jnp.dot(k_ref[rows, hd(h)], qm_ref[h], preferred_element_type=F32)
        for h in range(N_DIFF_HEADS):
            s = s_ref[h] - bias_ref[h]
            if diag:
                s = jnp.where(rel2 >= 0, s, NEG_BIG)
            shift = slopes_ref[h] * dist0
            m_old = m_ref[h]
            m_new = jnp.maximum(m_old, jnp.max(s, axis=0, keepdims=True) - shift)
            alpha = jnp.exp2(m_old - m_new)
            p = jnp.exp2(s - (m_new + shift))
            l_ref[h] = alpha * l_ref[h] + jnp.sum(p, axis=0, keepdims=True)
            p_ref[h] = p.astype(BF16)
            alpha_ref[h] = alpha
            m_ref[h] = m_new
        for h in range(N_DIFF_HEADS):
            acc_ref[h] = alpha_ref[h] * acc_ref[h] + jnp.dot(vT_ref[kj, hd(h), :], p_ref[h],
                                                               preferred_element_type=F32)

    def body(kj, c):
        step(kj, False)
        return c

    lax.fori_loop(0, qi, body, 0)
    step(qi, True)
    for h in range(N_DIFF_HEADS):
        w = acc_ref[h] / l_ref[h]
        out = w[:, :TB] - lam * w[:, TB:]
        ms = jnp.mean(out * out, axis=0, keepdims=True)
        o_ref[0, hd(h), :] = (out * lax.rsqrt(ms + RMS_EPS) * g_ref[...]).astype(BF16)


def _diff_attn(fm, tok, slopes, lam_params, g_col, lam_init, B, S):
    nq = S // TB
    T = B * S
    return pl.pallas_call(
        functools.partial(_diff_attn_kernel, lam_init),
        grid=(B, nq),
        in_specs=[
            pl.BlockSpec(memory_space=pltpu.SMEM),
            pl.BlockSpec((4, DIFF_QK_HALF), lambda b, q: (0, 0)),
            pl.BlockSpec((1, DIFF_W, TB), lambda b, q: (b * nq + q, 0, 0)),
            pl.BlockSpec((S, DIFF_W), lambda b, q: (b, 0)),
            pl.BlockSpec((nq, DIFF_W, TB), lambda b, q: (b, 1, 0)),
            pl.BlockSpec((DIFF_HEAD_DIM, 1), lambda b, q: (0, 0)),
        ],
        out_specs=pl.BlockSpec((1, DIFF_W, TB), lambda b, q: (b * nq + q, 0, 0)),
        out_shape=jax.ShapeDtypeStruct((T // TB, DIFF_W, TB), BF16),
        scratch_shapes=[
            pltpu.VMEM((N_DIFF_HEADS, DIFF_HEAD_DIM, 2 * TB), BF16),
            pltpu.VMEM((N_DIFF_HEADS, TB, 2 * TB), F32),
            pltpu.VMEM((N_DIFF_HEADS, 1, 2 * TB), F32),
            pltpu.VMEM((N_DIFF_HEADS, 1, 2 * TB), F32),
            pltpu.VMEM((N_DIFF_HEADS, DIFF_HEAD_DIM, 2 * TB), F32),
            pltpu.VMEM((N_DIFF_HEADS, TB, 2 * TB), F32),
            pltpu.VMEM((N_DIFF_HEADS, TB, 2 * TB), BF16),
            pltpu.VMEM((N_DIFF_HEADS, 1, 2 * TB), F32),
        ],
        compiler_params=_params(("parallel", "arbitrary")),
        name="diff_attn",
    )(slopes, lam_params, fm, tok, fm, g_col)


def _dsa_attn_kernel(topk, slopes_ref, sqT_ref, iqT_ref, svT_ref, sk_ref, ikk_ref, iwT_ref, o_ref, keys_ref,
                     iqall_ref, bias_ref, qall_ref, m_ref, l_ref, acc_ref):
    qi = pl.program_id(1)
    rel = _rel_pos()
    iw = iwT_ref[0]
    row = lax.broadcasted_iota(jnp.int32, (2 * IDX_HEAD_DIM, TB), 0)

    for p in range(N_IDX_HEADS // 2):
        pair = iqT_ref[0, p * 2 * IDX_HEAD_DIM:(p + 1) * 2 * IDX_HEAD_DIM, :]
        zero = jnp.zeros_like(pair)
        iqall_ref[:, (2 * p) * TB:(2 * p + 1) * TB] = jnp.where(row < IDX_HEAD_DIM, pair, zero)
        iqall_ref[:, (2 * p + 1) * TB:(2 * p + 2) * TB] = jnp.where(row >= IDX_HEAD_DIM, pair, zero)

    def score_block(kj, diag):
        rows = pl.ds(pl.multiple_of(kj * TB, TB), TB)
        ikk = ikk_ref[rows, :]
        acc = jnp.zeros((TB, TB), F32)
        half = N_IDX_HEADS // 2
        for g in range(2):
            r = jnp.dot(ikk, iqall_ref[:, g * half * TB:(g + 1) * half * TB], preferred_element_type=F32)
            for j in range(half):
                hh = g * half + j
                acc = acc + iw[hh:hh + 1, :] * jnp.maximum(r[:, j * TB:(j + 1) * TB], 0.0)
        if diag:
            acc = jnp.where(rel >= 0, acc, -jnp.inf)
        bits = pltpu.bitcast(acc, jnp.int32)
        keys_ref[rows, :] = jnp.where(bits >= 0, bits, bits ^ jnp.int32(0x7FFFFFFF))

    def score_body(kj, c):
        score_block(kj, False)
        return c

    lax.fori_loop(0, qi, score_body, 0)
    score_block(qi, True)

    def count_ge(cand):
        def body(kj, cnt):
            blk = keys_ref[pl.ds(pl.multiple_of(kj * TB, TB), TB), :]
            return cnt + jnp.sum(jnp.where(blk >= cand, 1, 0).astype(jnp.int32), axis=0, keepdims=True)
        return lax.fori_loop(0, qi + 1, body, jnp.zeros((1, TB), jnp.int32))

    lo = jnp.where(count_ge(jnp.zeros((1, TB), jnp.int32)) >= topk, jnp.int32(0), jnp.int32(INT_MIN))

    def bit_body(t, lo):
        cand = lo + lax.shift_left(jnp.int32(1), jnp.int32(30) - t)
        return jnp.where(count_ge(cand) >= topk, cand, lo)

    lo = lax.fori_loop(0, 31, bit_body, lo)
    thr = jnp.maximum(lo, jnp.int32(KEY_NEG_INF + 1))

    hd = lambda h: slice(h * DSA_HEAD_DIM, (h + 1) * DSA_HEAD_DIM)
    lanes = lambda h: slice(h * TB, (h + 1) * TB)
    _softmax_reset(m_ref, l_ref, acc_ref)
    for h in range(N_DSA_HEADS):
        bias_ref[:, lanes(h)] = slopes_ref[h] * rel
        qall_ref[:, lanes(h)] = sqT_ref[0, hd(h), :]
    slope_row = jnp.concatenate([jnp.full((1, TB), slopes_ref[h], F32) for h in range(N_DSA_HEADS)], axis=1)

    def attend(kj, c):
        rows = pl.ds(pl.multiple_of(kj * TB, TB), TB)
        dist0 = ((qi - kj) * TB).astype(F32)
        s = jnp.dot(sk_ref[rows, :], qall_ref[...], preferred_element_type=F32) - bias_ref[...]
        sel = keys_ref[rows, :] >= thr
        s = jnp.where(jnp.concatenate([sel] * N_DSA_HEADS, axis=1), s, NEG_BIG)
        shift = slope_row * dist0
        m_old = m_ref[0]
        m_new = jnp.maximum(m_old, jnp.max(s, axis=0, keepdims=True) - shift)
        m_ref[0] = m_new
        alpha = jnp.exp2(m_old - m_new)
        p = jnp.exp2(s - (m_new + shift))
        l_ref[0] = alpha * l_ref[0] + jnp.sum(p, axis=0, keepdims=True)
        acc_ref[0] = alpha * acc_ref[0] + jnp.dot(svT_ref[kj], p.astype(BF16), preferred_element_type=F32)
        return c

    lax.fori_loop(0, qi + 1, attend, 0)
    out = acc_ref[0] / l_ref[0]
    for h in range(N_DSA_HEADS):
        o_ref[0, hd(h), :] = out[:, lanes(h)].astype(BF16)


def _dsa_attn(fm, tok, iwT, slopes, B, S):
    nq = S // TB
    T = B * S
    topk = min(DSA_TOPK_MAX, S // 4)
    sq_blk = (2 * DIFF_W) // DSA_QW
    iq_blk = (2 * DIFF_W + DSA_QW) // IDX_QW
    sv_blk = (2 * DIFF_W + DSA_QW + IDX_QW) // DSA_HEAD_DIM
    sk_blk = DIFF_W // DSA_HEAD_DIM
    return pl.pallas_call(
        functools.partial(_dsa_attn_kernel, topk),
        grid=(B, nq),
        in_specs=[
            pl.BlockSpec(memory_space=pltpu.SMEM),
            pl.BlockSpec((1, DSA_QW, TB), lambda b, q: (b * nq + q, sq_blk, 0)),
            pl.BlockSpec((1, IDX_QW, TB), lambda b, q: (b * nq + q, iq_blk, 0)),
            pl.BlockSpec((nq, DSA_HEAD_DIM, TB), lambda b, q: (b, sv_blk, 0)),
            pl.BlockSpec((S, DSA_HEAD_DIM), lambda b, q: (b, sk_blk)),
            pl.BlockSpec((S, 2 * IDX_HEAD_DIM), lambda b, q: (b, sk_blk + 1)),
            pl.BlockSpec((1, N_IDX_HEADS, TB), lambda b, q: (b * nq + q, 0, 0)),
        ],
        out_specs=pl.BlockSpec((1, DSA_QW, TB), lambda b, q: (b * nq + q, 0, 0)),
        out_shape=jax.ShapeDtypeStruct((T // TB, DSA_QW, TB), BF16),
        scratch_shapes=[
            pltpu.VMEM((S, TB), jnp.int32),
            pltpu.VMEM((2 * IDX_HEAD_DIM, N_IDX_HEADS * TB), BF16),
            pltpu.VMEM((TB, N_DSA_HEADS * TB), F32),
            pltpu.VMEM((DSA_HEAD_DIM, N_DSA_HEADS * TB), BF16),
            pltpu.VMEM((1, 1, N_DSA_HEADS * TB), F32),
            pltpu.VMEM((1, 1, N_DSA_HEADS * TB), F32),
            pltpu.VMEM((1, DSA_HEAD_DIM, N_DSA_HEADS * TB), F32),
        ],
        compiler_params=_params(("parallel", "arbitrary")),
        name="dsa_attn",
    )(slopes, fm, fm, fm, tok, tok, iwT)


def _layer_norm_cols(y, g, b):
    mu = jnp.mean(y, axis=0, keepdims=True)
    d = y - mu
    var = jnp.mean(d * d, axis=0, keepdims=True)
    return d * lax.rsqrt(var + LN_EPS) * g + b


def _out_proj_kernel(alpha, diffT_ref, dsaT_ref, woT_ref, x_ref, g_ref, b_ref, x1T_ref, x1Tb_ref):
    nslab = diffT_ref.shape[0]
    attnT = jnp.concatenate(
        [jnp.concatenate([diffT_ref[s], dsaT_ref[s]], axis=0) for s in range(nslab)], axis=1)
    mixedT = jnp.dot(woT_ref[...], attnT, preferred_element_type=F32)
    for s in range(nslab):
        xT = x_ref[s * TB:(s + 1) * TB, :].T
        y = _layer_norm_cols(alpha * xT + mixedT[:, s * TB:(s + 1) * TB], g_ref[...], b_ref[...])
        x1T_ref[s] = y
        x1Tb_ref[s] = y.astype(BF16)


def _out_proj(diffT, dsaT, woT, x2d, g, b, alpha, tm):
    T, D = x2d.shape
    nslab = tm // TB
    fm = lambda i: (i, 0, 0)
    return pl.pallas_call(
        functools.partial(_out_proj_kernel, alpha),
        grid=(T // tm,),
        in_specs=[
            pl.BlockSpec((nslab, DIFF_W, TB), fm),
            pl.BlockSpec((nslab, DSA_QW, TB), fm),
            pl.BlockSpec((D, DIFF_W + DSA_QW), lambda i: (0, 0)),
            pl.BlockSpec((tm, D), lambda i: (i, 0)),
            pl.BlockSpec((D, 1), lambda i: (0, 0)),
            pl.BlockSpec((D, 1), lambda i: (0, 0)),
        ],
        out_specs=[pl.BlockSpec((nslab, D, TB), fm), pl.BlockSpec((nslab, D, TB), fm)],
        out_shape=[jax.ShapeDtypeStruct((T // TB, D, TB), F32), jax.ShapeDtypeStruct((T // TB, D, TB), BF16)],
        compiler_params=_params(("parallel",)),
        name="out_proj_ln1",
    )(diffT, dsaT, woT, x2d, g, b)


def _sorting_network(n):
    def merge(lo, hi, r):
        step = r * 2
        if step < hi - lo:
            yield from merge(lo, hi, step)
            yield from merge(lo + r, hi, step)
            yield from ((i, i + r) for i in range(lo + r, hi - r, step))
        else:
            yield (lo, lo + r)

    def sort(lo, hi):
        if hi - lo >= 1:
            mid = lo + (hi - lo) // 2
            yield from sort(lo, mid)
            yield from sort(mid + 1, hi)
            yield from merge(lo, hi, 1)
    return list(sort(0, n - 1))


def _top_sorted(s, k):
    assert s.shape[0] == 8 * k and k & (k - 1) == 0
    x = [s[g * 8:(g + 1) * 8] for g in range(k)]

    def exchange(i, j):
        x[i], x[j] = jnp.maximum(x[i], x[j]), jnp.minimum(x[i], x[j])

    for i, j in _sorting_network(k):
        exchange(i, j)
    for shift in (4, 2, 1):
        other = [pltpu.roll(v, shift, axis=0) for v in x]
        x = [jnp.maximum(x[i], other[k - 1 - i]) for i in range(k)]
        d = k // 2
        while d:
            for i in range(k):
                if not i & d:
                    exchange(i, i + d)
            d //= 2
    return [v[0:1] for v in x]


def _bf16_pair_words(x):
    bits = pltpu.bitcast(x, jnp.uint32)
    bits = bits + jnp.uint32(0x7FFF) + (lax.shift_right_logical(bits, jnp.uint32(16)) & jnp.uint32(1))
    hi = bits & jnp.uint32(0xFFFF0000)
    return hi | lax.shift_right_logical(hi, jnp.uint32(16))


def _peer_gate_kernel(x1Tb_ref, wqT_ref, k1_ref, k2_ref, cnt_ref, rnk_ref, e1_ref, e2_ref):
    nslab = x1Tb_ref.shape[0]
    x1cat = jnp.concatenate([x1Tb_ref[s] for s in range(nslab)], axis=1)
    qT = jnp.dot(wqT_ref[...], x1cat, preferred_element_type=F32).astype(BF16)
    for h in range(PEER_HEADS):
        base = h * PEER_QDIM
        s1w = jnp.dot(k1_ref[...], qT[base:base + PEER_HALF], preferred_element_type=F32)
        s2w = jnp.dot(k2_ref[...], qT[base + PEER_HALF:base + PEER_QDIM], preferred_element_type=F32)
        for s in range(nslab):
            s1 = s1w[:, s * TB:(s + 1) * TB]
            s2 = s2w[:, s * TB:(s + 1) * TB]
            v1 = _top_sorted(s1, PEER_TOPK)
            v2 = _top_sorted(s2, PEER_TOPK)
            rank2 = jnp.full(s2.shape, float(PEER_TOPK), F32)
            for b in reversed(range(PEER_TOPK)):
                rank2 = jnp.where(s2 == v2[b], float(b), rank2)
            v1c = jnp.concatenate(v1, axis=0)
            v2c = jnp.concatenate(v2, axis=0)
            half = PEER_TOPK // 2
            cand = jnp.concatenate(
                [v1[0] + v2c] + [v1[a] + v2c[:half] for a in range(1, half)] + [v1c[half:] + v2[0]], axis=0)
            pad = jnp.full((8 * PEER_TOPK - cand.shape[0], cand.shape[1]), -jnp.inf, F32)
            tau = _top_sorted(jnp.concatenate([cand, pad], axis=0), PEER_TOPK)[-1]
            m1, m2 = v1[0], v2[0]
            z = jnp.sum(jnp.where(cand >= tau, jnp.exp(cand - (m1 + m2)), 0.0), axis=0, keepdims=True)
            cnt_top = jnp.zeros(v1c.shape, F32)
            for vb in v2:
                cnt_top = cnt_top + jnp.where(v1c + vb >= tau, 1.0, 0.0)
            cnt = jnp.zeros(s1.shape, F32)
            for a in range(PEER_TOPK):
                cnt = jnp.where(s1 == v1[a], cnt_top[a:a + 1], cnt)
            rows = slice(h * PEER_NKEYS, (h + 1) * PEER_NKEYS)
            cnt_ref[s, rows, :] = _bf16_pair_words(cnt)
            rnk_ref[s, rows, :] = rank2.astype(BF16)
            e1_ref[s, rows, :] = _bf16_pair_words(jnp.exp(s1 - m1) / z)
            e2_ref[s, rows, :] = jnp.exp(s2 - m2).astype(BF16)


def _peer_gate(x1Tb, wqT, k1, k2, tm):
    nblk, D, _ = x1Tb.shape
    W = PEER_HEADS * PEER_NKEYS
    nslab = tm // TB
    fm = lambda i: (i, 0, 0)
    return pl.pallas_call(
        _peer_gate_kernel,
        grid=(nblk // nslab,),
        in_specs=[
            pl.BlockSpec((nslab, D, TB), fm),
            pl.BlockSpec((PEER_HEADS * PEER_QDIM, D), lambda i: (0, 0)),
            pl.BlockSpec((PEER_NKEYS, PEER_HALF), lambda i: (0, 0)),
            pl.BlockSpec((PEER_NKEYS, PEER_HALF), lambda i: (0, 0)),
        ],
        out_specs=[pl.BlockSpec((nslab, W, TB), fm)] * 4,
        out_shape=[jax.ShapeDtypeStruct((nblk, W, TB), dt) for dt in (jnp.uint32, BF16, jnp.uint32, BF16)],
        compiler_params=_params(("parallel",)),
        name="peer_gate",
    )(x1Tb, wqT, k1, k2)


def _peer_gate_units(i0, cnt_ref, rnk_ref, e1_ref, e2_ref, g_ref):
    def pair_row(ref, s, r):
        return pltpu.bitcast(jnp.broadcast_to(ref[s, r, :], (PEER_NKEYS // 2, TB)), BF16)

    def unit(ii, s):
        def run():
            i = jnp.minimum(i0 + ii, PEER_NKEYS - 1)
            gate = jnp.zeros((PEER_NKEYS, TB), BF16)
            for hd in range(PEER_HEADS):
                r = pl.ds(hd * PEER_NKEYS + i, 1)
                keys = slice(hd * PEER_NKEYS, (hd + 1) * PEER_NKEYS)
                e2 = e2_ref[s, keys, :]
                sel = jnp.where(rnk_ref[s, keys, :] < pair_row(cnt_ref, s, r), e2, jnp.zeros_like(e2))
                gate = gate + pair_row(e1_ref, s, r) * sel
            g_ref[ii * PEER_NKEYS:(ii + 1) * PEER_NKEYS, s * TB:(s + 1) * TB] = gate
        return run
    return [unit(ii, s) for ii in range(g_ref.shape[0] // PEER_NKEYS) for s in range(cnt_ref.shape[0])]


def _peer_dense_kernel(x1Tb_ref, u_ref, vT_ref, cnt_ref, rnk_ref, e1_ref, e2_ref, yT_ref,
                       ga_ref, gb_ref, h_ref, a_ref):
    ei = pl.program_id(1)
    te, tm = ga_ref.shape
    n_i = te // PEER_NKEYS
    nslab = cnt_ref.shape[0]
    gate_args = (cnt_ref, rnk_ref, e1_ref, e2_ref)

    @pl.when(ei == 0)
    def _():
        yT_ref[...] = jnp.zeros_like(yT_ref)
        for unit in _peer_gate_units(0, *gate_args, ga_ref):
            unit()

    D = u_ref.shape[1]
    kc = D // DENSE_H_CHUNKS
    rc = D // DENSE_Y_CHUNKS

    def h_unit(t, k):
        def run():
            xk = jnp.concatenate([x1Tb_ref[s, k * kc:(k + 1) * kc, :] for s in range(nslab)], axis=1)
            part = jnp.dot(u_ref[t * te:(t + 1) * te, k * kc:(k + 1) * kc], xk, preferred_element_type=F32)
            if k == 0:
                h_ref[t] = part
            else:
                h_ref[t] += part
        return run

    def act_unit(t, g_ref):
        def run():
            hT = h_ref[t]
            act = 0.5 * hT * (1.0 + lax.erf(hT * np.float32(1.0 / math.sqrt(2.0))))
            a_ref[t] = (act * g_ref[...].astype(F32)).astype(BF16)
        return run

    def y_unit(t, r):
        def run():
            y = jnp.dot(vT_ref[r * rc:(r + 1) * rc, t * te:(t + 1) * te], a_ref[t], preferred_element_type=F32)
            for s in range(nslab):
                yT_ref[s, r * rc:(r + 1) * rc, :] += y[:, s * TB:(s + 1) * TB]
        return run

    gb = _peer_gate_units((2 * ei + 1) * n_i, *gate_args, gb_ref)
    ga = _peer_gate_units((2 * ei + 2) * n_i, *gate_args, ga_ref)
    hA, hB = ([h_unit(t, k) for k in range(DENSE_H_CHUNKS)] for t in range(2))
    yA, yB = ([y_unit(t, r) for r in range(DENSE_Y_CHUNKS)] for t in range(2))
    actA, actB = act_unit(0, ga_ref), act_unit(1, gb_ref)
    mxu = hA + hB + yA + yB
    valu = gb[:len(gb) // 2] + [actA] + gb[len(gb) // 2:] + [actB] + ga
    need_before = {id(actA): hA[-1], id(actB): hB[-1], id(yA[0]): actA, id(yB[0]): actB}
    done, vi = set(), 0
    for n, m in enumerate(mxu):
        while id(m) in need_before and id(need_before[id(m)]) not in done:
            valu[vi]()
            done.add(id(valu[vi]))
            vi += 1
        m()
        done.add(id(m))
        target = -(-(n + 1) * len(valu) // len(mxu))
        while vi < target and not (id(valu[vi]) in need_before and id(need_before[id(valu[vi])]) not in done):
            valu[vi]()
            done.add(id(valu[vi]))
            vi += 1
    for v in valu[vi:]:
        v()


def _peer_dense(x1Tb, u, vT, gates, tm, te):
    nblk, D, _ = x1Tb.shape
    E = u.shape[0]
    W = PEER_HEADS * PEER_NKEYS
    nslab = tm // TB
    fm = lambda i, e: (i, 0, 0)
    return pl.pallas_call(
        _peer_dense_kernel,
        grid=(nblk // nslab, E // (2 * te)),
        in_specs=[
            pl.BlockSpec((nslab, D, TB), fm),
            pl.BlockSpec((2 * te, D), lambda i, e: (e, 0)),
            pl.BlockSpec((D, 2 * te), lambda i, e: (0, e)),
            pl.BlockSpec((nslab, W, TB), fm),
            pl.BlockSpec((nslab, W, TB), fm),
            pl.BlockSpec((nslab, W, TB), fm),
            pl.BlockSpec((nslab, W, TB), fm),
        ],
        out_specs=pl.BlockSpec((nslab, D, TB), fm),
        out_shape=jax.ShapeDtypeStruct((nblk, D, TB), F32),
        scratch_shapes=[pltpu.VMEM((te, tm), BF16), pltpu.VMEM((te, tm), BF16),
                        pltpu.VMEM((2, te, tm), F32), pltpu.VMEM((2, te, tm), BF16)],
        compiler_params=_params(("parallel", "arbitrary")),
        name="peer_dense",
    )(x1Tb, u, vT, *gates)


def _ln2_kernel(alpha, x1T_ref, yT_ref, g_ref, b_ref, o_ref):
    for s in range(x1T_ref.shape[0]):
        z = _layer_norm_cols(alpha * x1T_ref[s] + yT_ref[s], g_ref[...], b_ref[...])
        o_ref[s * TB:(s + 1) * TB, :] = z.T


def _ln2(x1T, yT, g, b, alpha, tm):
    nblk, D, _ = x1T.shape
    nslab = tm // TB
    fm = lambda i: (i, 0, 0)
    return pl.pallas_call(
        functools.partial(_ln2_kernel, alpha),
        grid=(nblk // nslab,),
        in_specs=[
            pl.BlockSpec((nslab, D, TB), fm),
            pl.BlockSpec((nslab, D, TB), fm),
            pl.BlockSpec((D, 1), lambda i: (0, 0)),
            pl.BlockSpec((D, 1), lambda i: (0, 0)),
        ],
        out_specs=pl.BlockSpec((tm, D), lambda i: (i, 0)),
        out_shape=jax.ShapeDtypeStruct((nblk * TB, D), F32),
        compiler_params=_params(("parallel",)),
        name="ln2_out",
    )(x1T, yT, g, b)


def _tiles(T):
    tm = 2 * TB if T % (2 * TB) == 0 else TB
    return dict(proj_tm=tm, proj_tn=FM_ROWS // 3, outproj_tm=tm, dense_tm=tm, dense_te=4 * PEER_NKEYS)


def kernel(x, w_in, w_o, lambda_q1, lambda_k1, lambda_q2, lambda_k2, subln_g, ln1_g, ln1_b,
           peer_wq, peer_k1, peer_k2, peer_u, peer_v, ln2_g, ln2_b):
    B, S, D = x.shape
    T = B * S
    depth = w_in.shape[0]
    assert S % TB == 0 and w_in.shape[2] == sum(IN_SPLITS)
    alpha = float((2 * depth) ** 0.25)
    slopes_diff, slopes_dsa = (s * F32(LOG2E) for s in _alibi_slopes())
    tiles = _tiles(T)
    offs = np.cumsum((0,) + IN_SPLITS)
    col = lambda w, k: w[:, offs[k]:offs[k + 1]]

    xt = x.reshape(T, D)
    for l in range(depth):
        w = w_in[l]
        dq, dk, dv, sq, sk, sv, iq, ik, iw = (col(w, k) for k in range(9))
        wt_fm = jnp.concatenate(
            [dq * (LOG2E * DIFF_QK_HALF ** -0.5), dv, sq * (LOG2E * DSA_HEAD_DIM ** -0.5), iq, sv],
            axis=1).T.astype(BF16)
        w_tok = jnp.concatenate([dk, sk, ik, ik], axis=1).astype(BF16)
        wt_iw = (iw * ((IDX_HEAD_DIM ** -0.5) * (N_IDX_HEADS ** -0.5))).T.astype(BF16)
        fm, tok, iwT = _in_proj(xt, wt_fm, w_tok, wt_iw, tiles["proj_tm"], tiles["proj_tn"])

        lam_init = 0.8 - 0.6 * math.exp(-0.3 * l)
        lam_params = jnp.stack([lambda_q1[l], lambda_k1[l], lambda_q2[l], lambda_k2[l]]).astype(F32)
        g_col = (subln_g[l].astype(F32) * (1.0 - lam_init)).reshape(DIFF_HEAD_DIM, 1)
        diffT = _diff_attn(fm, tok, slopes_diff, lam_params, g_col, lam_init, B, S)
        dsaT = _dsa_attn(fm, tok, iwT, slopes_dsa, B, S)

        colv = lambda p: p.reshape(D, 1).astype(F32)
        x1T, x1Tb = _out_proj(diffT, dsaT, w_o[l].T.astype(BF16), xt, colv(ln1_g[l]), colv(ln1_b[l]), alpha,
                              tiles["outproj_tm"])

        gates = _peer_gate(x1Tb, peer_wq[l].T.astype(BF16), peer_k1[l].astype(BF16), peer_k2[l].astype(BF16),
                           tiles["outproj_tm"])
        yT = _peer_dense(x1Tb, peer_u[l].astype(BF16), peer_v[l].T.astype(BF16), gates,
                         tiles["dense_tm"], tiles["dense_te"])
        xt = _ln2(x1T, yT, colv(ln2_g[l]), colv(ln2_b[l]), alpha, tiles["outproj_tm"])
    return xt.reshape(B, S, D)
```

```python
import functools
import math

import jax
import jax.numpy as jnp
import numpy as np
from jax import lax
from jax.experimental import pallas as pl
from jax.experimental.pallas import tpu as pltpu

F32 = jnp.float32
BF16 = jnp.bfloat16

N_DIFF_HEADS = 8
DIFF_HEAD_DIM = 128
DIFF_QK_HALF = DIFF_HEAD_DIM // 2
N_DSA_HEADS = 8
DSA_HEAD_DIM = 128
N_IDX_HEADS = 16
IDX_HEAD_DIM = 64
DSA_TOPK_MAX = 256
PEER_HEADS = 8
PEER_NKEYS = 128
PEER_QDIM = 256
PEER_HALF = PEER_QDIM // 2
PEER_TOPK = 16
LN_EPS = 1e-5
RMS_EPS = 1e-5

DIFF_W = N_DIFF_HEADS * DIFF_HEAD_DIM
DSA_QW = N_DSA_HEADS * DSA_HEAD_DIM
IDX_QW = N_IDX_HEADS * IDX_HEAD_DIM
IN_SPLITS = (DIFF_W, DIFF_W, DIFF_W, DSA_QW, DSA_HEAD_DIM, DSA_HEAD_DIM, IDX_QW, IDX_HEAD_DIM, N_IDX_HEADS)

TB = 256
FM_ROWS = DIFF_W * 2 + DSA_QW + IDX_QW + DSA_HEAD_DIM
TOK_COLS = DIFF_W + DSA_HEAD_DIM + 2 * IDX_HEAD_DIM
VMEM_LIMIT = 56 * 1024 * 1024
DENSE_H_CHUNKS = 4
DENSE_Y_CHUNKS = 4

LOG2E = math.log2(math.e)
NEG_BIG = -1e30
KEY_NEG_INF = (0xFF800000 ^ 0x7FFFFFFF) - 2 ** 32
INT_MIN = -(2 ** 31)

NT_DIMS = (((1,), (1,)), ((), ()))


def _alibi_slopes():
    n = N_DIFF_HEADS + N_DSA_HEADS
    s = 2.0 ** (-8.0 * np.arange(1, n + 1) / n)
    return (jnp.asarray(s[0::2], dtype=F32), jnp.asarray(s[1::2], dtype=F32))


def _params(sem):
    return pltpu.CompilerParams(dimension_semantics=sem, vmem_limit_bytes=VMEM_LIMIT)


def _in_proj_kernel(x_ref, wt_ref, wtok_ref, wiw_ref, fm_ref, tok_ref, iw_ref, xb_ref):
    nslab = fm_ref.shape[0]

    @pl.when(pl.program_id(1) == 0)
    def _():
        xb = x_ref[...].astype(BF16)
        xb_ref[...] = xb
        tok_ref[...] = jnp.dot(xb, wtok_ref[...], preferred_element_type=F32).astype(BF16)
        iw = lax.dot_general(wiw_ref[...], xb, NT_DIMS, preferred_element_type=F32)
        for s in range(nslab):
            iw_ref[s] = iw[:, s * TB:(s + 1) * TB]

    r = lax.dot_general(wt_ref[...], xb_ref[...], NT_DIMS, preferred_element_type=F32)
    for s in range(nslab):
        fm_ref[s] = r[:, s * TB:(s + 1) * TB].astype(BF16)


def _in_proj(x2d, wt_fm, w_tok, wt_iw, tm, tn):
    T, D = x2d.shape
    nslab = tm // TB
    return pl.pallas_call(
        _in_proj_kernel,
        grid=(T // tm, FM_ROWS // tn),
        in_specs=[
            pl.BlockSpec((tm, D), lambda i, j: (i, 0)),
            pl.BlockSpec((tn, D), lambda i, j: (j, 0)),
            pl.BlockSpec((D, TOK_COLS), lambda i, j: (0, 0)),
            pl.BlockSpec((N_IDX_HEADS, D), lambda i, j: (0, 0)),
        ],
        out_specs=[
            pl.BlockSpec((nslab, tn, TB), lambda i, j: (i, j, 0)),
            pl.BlockSpec((tm, TOK_COLS), lambda i, j: (i, 0)),
            pl.BlockSpec((nslab, N_IDX_HEADS, TB), lambda i, j: (i, 0, 0)),
        ],
        out_shape=[
            jax.ShapeDtypeStruct((T // TB, FM_ROWS, TB), BF16),
            jax.ShapeDtypeStruct((T, TOK_COLS), BF16),
            jax.ShapeDtypeStruct((T // TB, N_IDX_HEADS, TB), F32),
        ],
        scratch_shapes=[pltpu.VMEM((tm, D), BF16)],
        compiler_params=_params(("parallel", "arbitrary")),
        name="in_proj",
    )(x2d, wt_fm, w_tok, wt_iw)


def _rel_pos():
    kk = lax.broadcasted_iota(jnp.int32, (TB, TB), 0)
    qq = lax.broadcasted_iota(jnp.int32, (TB, TB), 1)
    return (qq - kk).astype(F32)


def _softmax_reset(m_ref, l_ref, acc_ref):
    m_ref[...] = jnp.full(m_ref.shape, NEG_BIG, F32)
    l_ref[...] = jnp.zeros(l_ref.shape, F32)
    acc_ref[...] = jnp.zeros(acc_ref.shape, F32)


def _diff_attn_kernel(lam_init, slopes_ref, lamp_ref, qT_ref, k_ref, vT_ref, g_ref, o_ref,
                      qm_ref, bias_ref, m_ref, l_ref, acc_ref, s_ref, p_ref, alpha_ref):
    qi = pl.program_id(1)
    lp = lamp_ref[...]
    lam = (jnp.exp(jnp.sum(lp[0:1] * lp[1:2], axis=1, keepdims=True))
           - jnp.exp(jnp.sum(lp[2:3] * lp[3:4], axis=1, keepdims=True)) + lam_init)
    rel = _rel_pos()
    row = lax.broadcasted_iota(jnp.int32, (DIFF_HEAD_DIM, TB), 0)
    hd = lambda h: slice(h * DIFF_HEAD_DIM, (h + 1) * DIFF_HEAD_DIM)

    _softmax_reset(m_ref, l_ref, acc_ref)
    rel2 = jnp.concatenate([rel, rel], axis=1)
    for h in range(N_DIFF_HEADS):
        qT = qT_ref[0, hd(h), :]
        zero = jnp.zeros_like(qT)
        qm_ref[h] = jnp.concatenate(
            [jnp.where(row < DIFF_QK_HALF, qT, zero), jnp.where(row >= DIFF_QK_HALF, qT, zero)], axis=1)
        bias_ref[h] = slopes_ref[h] * rel2

    def step(kj, diag):
        rows = pl.ds(pl.multiple_of(kj * TB, TB), TB)
        dist0 = ((qi - kj) * TB).astype(F32)
        for h in range(N_DIFF_HEADS):
            s_ref[h] = jnp.dot(k_ref[rows, hd(h)], qm_ref[h], preferred_element_type=F32)
        for h in range(N_DIFF_HEADS):
            s = s_ref[h] - bias_ref[h]
            if diag:
                s = jnp.where(rel2 >= 0, s, NEG_BIG)
            shift = slopes_ref[h] * dist0
            m_old = m_ref[h]
            m_new = jnp.maximum(m_old, jnp.max(s, axis=0, keepdims=True) - shift)
            alpha = jnp.exp2(m_old - m_new)
            p = jnp.exp2(s - (m_new + shift))
            l_ref[h] = alpha * l_ref[h] + jnp.sum(p, axis=0, keepdims=True)
            p_ref[h] = p.astype(BF16)
            alpha_ref[h] = alpha
            m_ref[h] = m_new
        for h in range(N_DIFF_HEADS):
            acc_ref[h] = alpha_ref[h] * acc_ref[h] + jnp.dot(vT_ref[kj, hd(h), :], p_ref[h],
                                                               preferred_element_type=F32)

    def body(kj, c):
        step(kj, False)
        return c

    lax.fori_loop(0, qi, body, 0)
    step(qi, True)
    for h in range(N_DIFF_HEADS):
        w = acc_ref[h] / l_ref[h]
        out = w[:, :TB] - lam * w[:, TB:]
        ms = jnp.mean(out * out, axis=0, keepdims=True)
        o_ref[0, hd(h), :] = (out * lax.rsqrt(ms + RMS_EPS) * g_ref[...]).astype(BF16)


def _diff_attn(fm, tok, slopes, lam_params, g_col, lam_init, B, S):
    nq = S // TB
    T = B * S
    return pl.pallas_call(
        functools.partial(_diff_attn_kernel, lam_init),
        grid=(B, nq),
        in_specs=[
            pl.BlockSpec(memory_space=pltpu.SMEM),
            pl.BlockSpec((4, DIFF_QK_HALF), lambda b, q: (0, 0)),
            pl.BlockSpec((1, DIFF_W, TB), lambda b, q: (b * nq + q, 0, 0)),
            pl.BlockSpec((S, DIFF_W), lambda b, q: (b, 0)),
            pl.BlockSpec((nq, DIFF_W, TB), lambda b, q: (b, 1, 0)),
            pl.BlockSpec((DIFF_HEAD_DIM, 1), lambda b, q: (0, 0)),
        ],
        out_specs=pl.BlockSpec((1, DIFF_W, TB), lambda b, q: (b * nq + q, 0, 0)),
        out_shape=jax.ShapeDtypeStruct((T // TB, DIFF_W, TB), BF16),
        scratch_shapes=[
            pltpu.VMEM((N_DIFF_HEADS, DIFF_HEAD_DIM, 2 * TB), BF16),
            pltpu.VMEM((N_DIFF_HEADS, TB, 2 * TB), F32),
            pltpu.VMEM((N_DIFF_HEADS, 1, 2 * TB), F32),
            pltpu.VMEM((N_DIFF_HEADS, 1, 2 * TB), F32),
            pltpu.VMEM((N_DIFF_HEADS, DIFF_HEAD_DIM, 2 * TB), F32),
            pltpu.VMEM((N_DIFF_HEADS, TB, 2 * TB), F32),
            pltpu.VMEM((N_DIFF_HEADS, TB, 2 * TB), BF16),
            pltpu.VMEM((N_DIFF_HEADS, 1, 2 * TB), F32),
        ],
        compiler_params=_params(("parallel", "arbitrary")),
        name="diff_attn",
    )(slopes, lam_params, fm, tok, fm, g_col)


def _dsa_attn_kernel(topk, slopes_ref, sqT_ref, iqT_ref, svT_ref, sk_ref, ikk_ref, iwT_ref, o_ref, keys_ref,
                     iqall_ref, bias_ref, qall_ref, m_ref, l_ref, acc_ref):
    qi = pl.program_id(1)
    rel = _rel_pos()
    iw = iwT_ref[0]
    row = lax.broadcasted_iota(jnp.int32, (2 * IDX_HEAD_DIM, TB), 0)

    for p in range(N_IDX_HEADS // 2):
        pair = iqT_ref[0, p * 2 * IDX_HEAD_DIM:(p + 1) * 2 * IDX_HEAD_DIM, :]
        zero = jnp.zeros_like(pair)
        iqall_ref[:, (2 * p) * TB:(2 * p + 1) * TB] = jnp.where(row < IDX_HEAD_DIM, pair, zero)
        iqall_ref[:, (2 * p + 1) * TB:(2 * p + 2) * TB] = jnp.where(row >= IDX_HEAD_DIM, pair, zero)

    def score_block(kj, diag):
        rows = pl.ds(pl.multiple_of(kj * TB, TB), TB)
        ikk = ikk_ref[rows, :]
        acc = jnp.zeros((TB, TB), F32)
        half = N_IDX_HEADS // 2
        for g in range(2):
            r = jnp.dot(ikk, iqall_ref[:, g * half * TB:(g + 1) * half * TB], preferred_element_type=F32)
            for j in range(half):
                hh = g * half + j
                acc = acc + iw[hh:hh + 1, :] * jnp.maximum(r[:, j * TB:(j + 1) * TB], 0.0)
        if diag:
            acc = jnp.where(rel >= 0, acc, -jnp.inf)
        bits = pltpu.bitcast(acc, jnp.int32)
        keys_ref[rows, :] = jnp.where(bits >= 0, bits, bits ^ jnp.int32(0x7FFFFFFF))

    def score_body(kj, c):
        score_block(kj, False)
        return c

    lax.fori_loop(0, qi, score_body, 0)
    score_block(qi, True)

    def count_ge(cand):
        def body(kj, cnt):
            blk = keys_ref[pl.ds(pl.multiple_of(kj * TB, TB), TB), :]
            return cnt + jnp.sum(jnp.where(blk >= cand, 1, 0).astype(jnp.int32), axis=0, keepdims=True)
        return lax.fori_loop(0, qi + 1, body, jnp.zeros((1, TB), jnp.int32))

    lo = jnp.where(count_ge(jnp.zeros((1, TB), jnp.int32)) >= topk, jnp.int32(0), jnp.int32(INT_MIN))

    def bit_body(t, lo):
        cand = lo + lax.shift_left(jnp.int32(1), jnp.int32(30) - t)
        return jnp.where(count_ge(cand) >= topk, cand, lo)

    lo = lax.fori_loop(0, 31, bit_body, lo)
    thr = jnp.maximum(lo, jnp.int32(KEY_NEG_INF + 1))

    hd = lambda h: slice(h * DSA_HEAD_DIM, (h + 1) * DSA_HEAD_DIM)
    lanes = lambda h: slice(h * TB, (h + 1) * TB)
    _softmax_reset(m_ref, l_ref, acc_ref)
    for h in range(N_DSA_HEADS):
        bias_ref[:, lanes(h)] = slopes_ref[h] * rel
        qall_ref[:, lanes(h)] = sqT_ref[0, hd(h), :]
    slope_row = jnp.concatenate([jnp.full((1, TB), slopes_ref[h], F32) for h in range(N_DSA_HEADS)], axis=1)

    def attend(kj, c):
        rows = pl.ds(pl.multiple_of(kj * TB, TB), TB)
        dist0 = ((qi - kj) * TB).astype(F32)
        s = jnp.dot(sk_ref[rows, :], qall_ref[...], preferred_element_type=F32) - bias_ref[...]
        sel = keys_ref[rows, :] >= thr
        s = jnp.where(jnp.concatenate([sel] * N_DSA_HEADS, axis=1), s, NEG_BIG)
        shift = slope_row * dist0
        m_old = m_ref[0]
        m_new = jnp.maximum(m_old, jnp.max(s, axis=0, keepdims=True) - shift)
        m_ref[0] = m_new
        alpha = jnp.exp2(m_old - m_new)
        p = jnp.exp2(s - (m_new + shift))
        l_ref[0] = alpha * l_ref[0] + jnp.sum(p, axis=0, keepdims=True)
        acc_ref[0] = alpha * acc_ref[0] + jnp.dot(svT_ref[kj], p.astype(BF16), preferred_element_type=F32)
        return c

    lax.fori_loop(0, qi + 1, attend, 0)
    out = acc_ref[0] / l_ref[0]
    for h in range(N_DSA_HEADS):
        o_ref[0, hd(h), :] = out[:, lanes(h)].astype(BF16)


def _dsa_attn(fm, tok, iwT, slopes, B, S):
    nq = S // TB
    T = B * S
    topk = min(DSA_TOPK_MAX, S // 4)
    sq_blk = (2 * DIFF_W) // DSA_QW
    iq_blk = (2 * DIFF_W + DSA_QW) // IDX_QW
    sv_blk = (2 * DIFF_W + DSA_QW + IDX_QW) // DSA_HEAD_DIM
    sk_blk = DIFF_W // DSA_HEAD_DIM
    return pl.pallas_call(
        functools.partial(_dsa_attn_kernel, topk),
        grid=(B, nq),
        in_specs=[
            pl.BlockSpec(memory_space=pltpu.SMEM),
            pl.BlockSpec((1, DSA_QW, TB), lambda b, q: (b * nq + q, sq_blk, 0)),
            pl.BlockSpec((1, IDX_QW, TB), lambda b, q: (b * nq + q, iq_blk, 0)),
            pl.BlockSpec((nq, DSA_HEAD_DIM, TB), lambda b, q: (b, sv_blk, 0)),
            pl.BlockSpec((S, DSA_HEAD_DIM), lambda b, q: (b, sk_blk)),
            pl.BlockSpec((S, 2 * IDX_HEAD_DIM), lambda b, q: (b, sk_blk + 1)),
            pl.BlockSpec((1, N_IDX_HEADS, TB), lambda b, q: (b * nq + q, 0, 0)),
        ],
        out_specs=pl.BlockSpec((1, DSA_QW, TB), lambda b, q: (b * nq + q, 0, 0)),
        out_shape=jax.ShapeDtypeStruct((T // TB, DSA_QW, TB), BF16),
        scratch_shapes=[
            pltpu.VMEM((S, TB), jnp.int32),
            pltpu.VMEM((2 * IDX_HEAD_DIM, N_IDX_HEADS * TB), BF16),
            pltpu.VMEM((TB, N_DSA_HEADS * TB), F32),
            pltpu.VMEM((DSA_HEAD_DIM, N_DSA_HEADS * TB), BF16),
            pltpu.VMEM((1, 1, N_DSA_HEADS * TB), F32),
            pltpu.VMEM((1, 1, N_DSA_HEADS * TB), F32),
            pltpu.VMEM((1, DSA_HEAD_DIM, N_DSA_HEADS * TB), F32),
        ],
        compiler_params=_params(("parallel", "arbitrary")),
        name="dsa_attn",
    )(slopes, fm, fm, fm, tok, tok, iwT)


def _layer_norm_cols(y, g, b):
    mu = jnp.mean(y, axis=0, keepdims=True)
    d = y - mu
    var = jnp.mean(d * d, axis=0, keepdims=True)
    return d * lax.rsqrt(var + LN_EPS) * g + b


def _out_proj_kernel(alpha, diffT_ref, dsaT_ref, woT_ref, x_ref, g_ref, b_ref, x1T_ref, x1Tb_ref):
    nslab = diffT_ref.shape[0]
    attnT = jnp.concatenate(
        [jnp.concatenate([diffT_ref[s], dsaT_ref[s]], axis=0) for s in range(nslab)], axis=1)
    mixedT = jnp.dot(woT_ref[...], attnT, preferred_element_type=F32)
    for s in range(nslab):
        xT = x_ref[s * TB:(s + 1) * TB, :].T
        y = _layer_norm_cols(alpha * xT + mixedT[:, s * TB:(s + 1) * TB], g_ref[...], b_ref[...])
        x1T_ref[s] = y
        x1Tb_ref[s] = y.astype(BF16)


def _out_proj(diffT, dsaT, woT, x2d, g, b, alpha, tm):
    T, D = x2d.shape
    nslab = tm // TB
    fm = lambda i: (i, 0, 0)
    return pl.pallas_call(
        functools.partial(_out_proj_kernel, alpha),
        grid=(T // tm,),
        in_specs=[
            pl.BlockSpec((nslab, DIFF_W, TB), fm),
            pl.BlockSpec((nslab, DSA_QW, TB), fm),
            pl.BlockSpec((D, DIFF_W + DSA_QW), lambda i: (0, 0)),
            pl.BlockSpec((tm, D), lambda i: (i, 0)),
            pl.BlockSpec((D, 1), lambda i: (0, 0)),
            pl.BlockSpec((D, 1), lambda i: (0, 0)),
        ],
        out_specs=[pl.BlockSpec((nslab, D, TB), fm), pl.BlockSpec((nslab, D, TB), fm)],
        out_shape=[jax.ShapeDtypeStruct((T // TB, D, TB), F32), jax.ShapeDtypeStruct((T // TB, D, TB), BF16)],
        compiler_params=_params(("parallel",)),
        name="out_proj_ln1",
    )(diffT, dsaT, woT, x2d, g, b)


def _sorting_network(n):
    def merge(lo, hi, r):
        step = r * 2
        if step < hi - lo:
            yield from merge(lo, hi, step)
            yield from merge(lo + r, hi, step)
            yield from ((i, i + r) for i in range(lo + r, hi - r, step))
        else:
            yield (lo, lo + r)

    def sort(lo, hi):
        if hi - lo >= 1:
            mid = lo + (hi - lo) // 2
            yield from sort(lo, mid)
            yield from sort(mid + 1, hi)
            yield from merge(lo, hi, 1)
    return list(sort(0, n - 1))


def _top_sorted(s, k):
    assert s.shape[0] == 8 * k and k & (k - 1) == 0
    x = [s[g * 8:(g + 1) * 8] for g in range(k)]

    def exchange(i, j):
        x[i], x[j] = jnp.maximum(x[i], x[j]), jnp.minimum(x[i], x[j])

    for i, j in _sorting_network(k):
        exchange(i, j)
    for shift in (4, 2, 1):
        other = [pltpu.roll(v, shift, axis=0) for v in x]
        x = [jnp.maximum(x[i], other[k - 1 - i]) for i in range(k)]
        d = k // 2
        while d:
            for i in range(k):
                if not i & d:
                    exchange(i, i + d)
            d //= 2
    return [v[0:1] for v in x]


def _bf16_pair_words(x):
    bits = pltpu.bitcast(x, jnp.uint32)
    bits = bits + jnp.uint32(0x7FFF) + (lax.shift_right_logical(bits, jnp.uint32(16)) & jnp.uint32(1))
    hi = bits & jnp.uint32(0xFFFF0000)
    return hi | lax.shift_right_logical(hi, jnp.uint32(16))


def _peer_gate_kernel(x1Tb_ref, wqT_ref, k1_ref, k2_ref, cnt_ref, rnk_ref, e1_ref, e2_ref):
    nslab = x1Tb_ref.shape[0]
    x1cat = jnp.concatenate([x1Tb_ref[s] for s in range(nslab)], axis=1)
    qT = jnp.dot(wqT_ref[...], x1cat, preferred_element_type=F32).astype(BF16)
    for h in range(PEER_HEADS):
        base = h * PEER_QDIM
        s1w = jnp.dot(k1_ref[...], qT[base:base + PEER_HALF], preferred_element_type=F32)
        s2w = jnp.dot(k2_ref[...], qT[base + PEER_HALF:base + PEER_QDIM], preferred_element_type=F32)
        for s in range(nslab):
            s1 = s1w[:, s * TB:(s + 1) * TB]
            s2 = s2w[:, s * TB:(s + 1) * TB]
            v1 = _top_sorted(s1, PEER_TOPK)
            v2 = _top_sorted(s2, PEER_TOPK)
            rank2 = jnp.full(s2.shape, float(PEER_TOPK), F32)
            for b in reversed(range(PEER_TOPK)):
                rank2 = jnp.where(s2 == v2[b], float(b), rank2)
            v1c = jnp.concatenate(v1, axis=0)
            v2c = jnp.concatenate(v2, axis=0)
            half = PEER_TOPK // 2
            cand = jnp.concatenate(
                [v1[0] + v2c] + [v1[a] + v2c[:half] for a in range(1, half)] + [v1c[half:] + v2[0]], axis=0)
            pad = jnp.full((8 * PEER_TOPK - cand.shape[0], cand.shape[1]), -jnp.inf, F32)
            tau = _top_sorted(jnp.concatenate([cand, pad], axis=0), PEER_TOPK)[-1]
            m1, m2 = v1[0], v2[0]
            z = jnp.sum(jnp.where(cand >= tau, jnp.exp(cand - (m1 + m2)), 0.0), axis=0, keepdims=True)
            cnt_top = jnp.zeros(v1c.shape, F32)
            for vb in v2:
                cnt_top = cnt_top + jnp.where(v1c + vb >= tau, 1.0, 0.0)
            cnt = jnp.zeros(s1.shape, F32)
            for a in range(PEER_TOPK):
                cnt = jnp.where(s1 == v1[a], cnt_top[a:a + 1], cnt)
            rows = slice(h * PEER_NKEYS, (h + 1) * PEER_NKEYS)
            cnt_ref[s, rows, :] = _bf16_pair_words(cnt)
            rnk_ref[s, rows, :] = rank2.astype(BF16)
            e1_ref[s, rows, :] = _bf16_pair_words(jnp.exp(s1 - m1) / z)
            e2_ref[s, rows, :] = jnp.exp(s2 - m2).astype(BF16)


def _peer_gate(x1Tb, wqT, k1, k2, tm):
    nblk, D, _ = x1Tb.shape
    W = PEER_HEADS * PEER_NKEYS
    nslab = tm // TB
    fm = lambda i: (i, 0, 0)
    return pl.pallas_call(
        _peer_gate_kernel,
        grid=(nblk // nslab,),
        in_specs=[
            pl.BlockSpec((nslab, D, TB), fm),
            pl.BlockSpec((PEER_HEADS * PEER_QDIM, D), lambda i: (0, 0)),
            pl.BlockSpec((PEER_NKEYS, PEER_HALF), lambda i: (0, 0)),
            pl.BlockSpec((PEER_NKEYS, PEER_HALF), lambda i: (0, 0)),
        ],
        out_specs=[pl.BlockSpec((nslab, W, TB), fm)] * 4,
        out_shape=[jax.ShapeDtypeStruct((nblk, W, TB), dt) for dt in (jnp.uint32, BF16, jnp.uint32, BF16)],
        compiler_params=_params(("parallel",)),
        name="peer_gate",
    )(x1Tb, wqT, k1, k2)


def _peer_gate_units(i0, cnt_ref, rnk_ref, e1_ref, e2_ref, g_ref):
    def pair_row(ref, s, r):
        return pltpu.bitcast(jnp.broadcast_to(ref[s, r, :], (PEER_NKEYS // 2, TB)), BF16)

    def unit(ii, s):
        def run():
            i = jnp.minimum(i0 + ii, PEER_NKEYS - 1)
            gate = jnp.zeros((PEER_NKEYS, TB), BF16)
            for hd in range(PEER_HEADS):
                r = pl.ds(hd * PEER_NKEYS + i, 1)
                keys = slice(hd * PEER_NKEYS, (hd + 1) * PEER_NKEYS)
                e2 = e2_ref[s, keys, :]
                sel = jnp.where(rnk_ref[s, keys, :] < pair_row(cnt_ref, s, r), e2, jnp.zeros_like(e2))
                gate = gate + pair_row(e1_ref, s, r) * sel
            g_ref[ii * PEER_NKEYS:(ii + 1) * PEER_NKEYS, s * TB:(s + 1) * TB] = gate
        return run
    return [unit(ii, s) for ii in range(g_ref.shape[0] // PEER_NKEYS) for s in range(cnt_ref.shape[0])]


def _peer_dense_kernel(x1Tb_ref, u_ref, vT_ref, cnt_ref, rnk_ref, e1_ref, e2_ref, yT_ref,
                       ga_ref, gb_ref, h_ref, a_ref):
    ei = pl.program_id(1)
    te, tm = ga_ref.shape
    n_i = te // PEER_NKEYS
    nslab = cnt_ref.shape[0]
    gate_args = (cnt_ref, rnk_ref, e1_ref, e2_ref)

    @pl.when(ei == 0)
    def _():
        yT_ref[...] = jnp.zeros_like(yT_ref)
        for unit in _peer_gate_units(0, *gate_args, ga_ref):
            unit()

    D = u_ref.shape[1]
    kc = D // DENSE_H_CHUNKS
    rc = D // DENSE_Y_CHUNKS

    def h_unit(t, k):
        def run():
            xk = jnp.concatenate([x1Tb_ref[s, k * kc:(k + 1) * kc, :] for s in range(nslab)], axis=1)
            part = jnp.dot(u_ref[t * te:(t + 1) * te, k * kc:(k + 1) * kc], xk, preferred_element_type=F32)
            if k == 0:
                h_ref[t] = part
            else:
                h_ref[t] += part
        return run

    def act_unit(t, g_ref):
        def run():
            hT = h_ref[t]
            act = 0.5 * hT * (1.0 + lax.erf(hT * np.float32(1.0 / math.sqrt(2.0))))
            a_ref[t] = (act * g_ref[...].astype(F32)).astype(BF16)
        return run

    def y_unit(t, r):
        def run():
            y = jnp.dot(vT_ref[r * rc:(r + 1) * rc, t * te:(t + 1) * te], a_ref[t], preferred_element_type=F32)
            for s in range(nslab):
                yT_ref[s, r * rc:(r + 1) * rc, :] += y[:, s * TB:(s + 1) * TB]
        return run

    gb = _peer_gate_units((2 * ei + 1) * n_i, *gate_args, gb_ref)
    ga = _peer_gate_units((2 * ei + 2) * n_i, *gate_args, ga_ref)
    hA, hB = ([h_unit(t, k) for k in range(DENSE_H_CHUNKS)] for t in range(2))
    yA, yB = ([y_unit(t, r) for r in range(DENSE_Y_CHUNKS)] for t in range(2))
    actA, actB = act_unit(0, ga_ref), act_unit(1, gb_ref)
    mxu = hA + hB + yA + yB
    valu = gb[:len(gb) // 2] + [actA] + gb[len(gb) // 2:] + [actB] + ga
    need_before = {id(actA): hA[-1], id(actB): hB[-1], id(yA[0]): actA, id(yB[0]): actB}
    done, vi = set(), 0
    for n, m in enumerate(mxu):
        while id(m) in need_before and id(need_before[id(m)]) not in done:
            valu[vi]()
            done.add(id(valu[vi]))
            vi += 1
        m()
        done.add(id(m))
        target = -(-(n + 1) * len(valu) // len(mxu))
        while vi < target and not (id(valu[vi]) in need_before and id(need_before[id(valu[vi])]) not in done):
            valu[vi]()
            done.add(id(valu[vi]))
            vi += 1
    for v in valu[vi:]:
        v()


def _peer_dense(x1Tb, u, vT, gates, tm, te):
    nblk, D, _ = x1Tb.shape
    E = u.shape[0]
    W = PEER_HEADS * PEER_NKEYS
    nslab = tm // TB
    fm = lambda i, e: (i, 0, 0)
    return pl.pallas_call(
        _peer_dense_kernel,
        grid=(nblk // nslab, E // (2 * te)),
        in_specs=[
            pl.BlockSpec((nslab, D, TB), fm),
            pl.BlockSpec((2 * te, D), lambda i, e: (e, 0)),
            pl.BlockSpec((D, 2 * te), lambda i, e: (0, e)),
            pl.BlockSpec((nslab, W, TB), fm),
            pl.BlockSpec((nslab, W, TB), fm),
            pl.BlockSpec((nslab, W, TB), fm),
            pl.BlockSpec((nslab, W, TB), fm),
        ],
        out_specs=pl.BlockSpec((nslab, D, TB), fm),
        out_shape=jax.ShapeDtypeStruct((nblk, D, TB), F32),
        scratch_shapes=[pltpu.VMEM((te, tm), BF16), pltpu.VMEM((te, tm), BF16),
                        pltpu.VMEM((2, te, tm), F32), pltpu.VMEM((2, te, tm), BF16)],
        compiler_params=_params(("parallel", "arbitrary")),
        name="peer_dense",
    )(x1Tb, u, vT, *gates)


def _ln2_kernel(alpha, x1T_ref, yT_ref, g_ref, b_ref, o_ref):
    for s in range(x1T_ref.shape[0]):
        z = _layer_norm_cols(alpha * x1T_ref[s] + yT_ref[s], g_ref[...], b_ref[...])
        o_ref[s * TB:(s + 1) * TB, :] = z.T


def _ln2(x1T, yT, g, b, alpha, tm):
    nblk, D, _ = x1T.shape
    nslab = tm // TB
    fm = lambda i: (i, 0, 0)
    return pl.pallas_call(
        functools.partial(_ln2_kernel, alpha),
        grid=(nblk // nslab,),
        in_specs=[
            pl.BlockSpec((nslab, D, TB), fm),
            pl.BlockSpec((nslab, D, TB), fm),
            pl.BlockSpec((D, 1), lambda i: (0, 0)),
            pl.BlockSpec((D, 1), lambda i: (0, 0)),
        ],
        out_specs=pl.BlockSpec((tm, D), lambda i: (i, 0)),
        out_shape=jax.ShapeDtypeStruct((nblk * TB, D), F32),
        compiler_params=_params(("parallel",)),
        name="ln2_out",
    )(x1T, yT, g, b)


def _tiles(T):
    tm = 2 * TB if T % (2 * TB) == 0 else TB
    return dict(proj_tm=tm, proj_tn=FM_ROWS // 2, outproj_tm=tm, dense_tm=tm, dense_te=4 * PEER_NKEYS)


def kernel(x, w_in, w_o, lambda_q1, lambda_k1, lambda_q2, lambda_k2, subln_g, ln1_g, ln1_b,
           peer_wq, peer_k1, peer_k2, peer_u, peer_v, ln2_g, ln2_b):
    B, S, D = x.shape
    T = B * S
    depth = w_in.shape[0]
    assert S % TB == 0 and w_in.shape[2] == sum(IN_SPLITS)
    alpha = float((2 * depth) ** 0.25)
    slopes_diff, slopes_dsa = (s * F32(LOG2E) for s in _alibi_slopes())
    tiles = _tiles(T)
    offs = np.cumsum((0,) + IN_SPLITS)
    col = lambda w, k: w[:, offs[k]:offs[k + 1]]

    xt = x.reshape(T, D)
    for l in range(depth):
        w = w_in[l]
        dq, dk, dv, sq, sk, sv, iq, ik, iw = (col(w, k) for k in range(9))
        wt_fm = jnp.concatenate(
            [dq * (LOG2E * DIFF_QK_HALF ** -0.5), dv, sq * (LOG2E * DSA_HEAD_DIM ** -0.5), iq, sv],
            axis=1).T.astype(BF16)
        w_tok = jnp.concatenate([dk, sk, ik, ik], axis=1).astype(BF16)
        wt_iw = (iw * ((IDX_HEAD_DIM ** -0.5) * (N_IDX_HEADS ** -0.5))).T.astype(BF16)
        fm, tok, iwT = _in_proj(xt, wt_fm, w_tok, wt_iw, tiles["proj_tm"], tiles["proj_tn"])

        lam_init = 0.8 - 0.6 * math.exp(-0.3 * l)
        lam_params = jnp.stack([lambda_q1[l], lambda_k1[l], lambda_q2[l], lambda_k2[l]]).astype(F32)
        g_col = (subln_g[l].astype(F32) * (1.0 - lam_init)).reshape(DIFF_HEAD_DIM, 1)
        diffT = _diff_attn(fm, tok, slopes_diff, lam_params, g_col, lam_init, B, S)
        dsaT = _dsa_attn(fm, tok, iwT, slopes_dsa, B, S)

        colv = lambda p: p.reshape(D, 1).astype(F32)
        x1T, x1Tb = _out_proj(diffT, dsaT, w_o[l].T.astype(BF16), xt, colv(ln1_g[l]), colv(ln1_b[l]), alpha,
                              tiles["outproj_tm"])

        gates = _peer_gate(x1Tb, peer_wq[l].T.astype(BF16), peer_k1[l].astype(BF16), peer_k2[l].astype(BF16),
                           tiles["outproj_tm"])
        yT = _peer_dense(x1Tb, peer_u[l].astype(BF16), peer_v[l].T.astype(BF16), gates,
                         tiles["dense_tm"], tiles["dense_te"])
        xt = _ln2(x1T, yT, colv(ln2_g[l]), colv(ln2_b[l]), alpha, tiles["outproj_tm"])
    return xt.reshape(B, S, D)
```
